```python
import math
import jax, jax.numpy as jnp
from jax import lax
import numpy as np

D_MODEL = 1024
BATCH = 16
SEQ = 4096
DEPTH = 4

PLE_DIM = 256
HEAD_DIM = 64
N_Q_HEADS = 8
N_KV_HEADS = 2
GQA_GROUP = N_Q_HEADS // N_KV_HEADS
WINDOW = 128
BLOCK = 128
ROPE_THETA = 500000.0
ROPE_DIM = HEAD_DIM // 4
Q_WIDTH = N_Q_HEADS * HEAD_DIM
KV_WIDTH = N_KV_HEADS * HEAD_DIM
SSM_WIDTH = D_MODEL // 4
SSM_GROUP = 16
SSM_GROUPS = SSM_WIDTH // SSM_GROUP
SSM_STATE = 64
CONV_WIDTH = D_MODEL // 4
CONV_K = 31
FFN_HIDDEN = ((-(-8 * D_MODEL // 3) + 255) // 256) * 256
N_BRANCH = 3
IN_WIDTHS = [Q_WIDTH, KV_WIDTH, KV_WIDTH, SSM_WIDTH, 2 * CONV_WIDTH, N_BRANCH * D_MODEL]
IN_WIDTH = sum(IN_WIDTHS)
SPLIT_POINTS = [int(v) for v in np.cumsum(IN_WIDTHS)[:-1]]
EPS = 1e-6
NEG_INF = -1e30

kernel_name = "hybrid_gated_swa_s5_conformer_block"


def rms_norm(t, g):
    t32 = t.astype(jnp.float32)
    out = t32 * lax.rsqrt(jnp.mean(t32 * t32, axis=-1, keepdims=True) + EPS) * g.astype(jnp.float32)
    return out.astype(t.dtype)


def layer_norm(t, g, b):
    t32 = t.astype(jnp.float32)
    mu = jnp.mean(t32, axis=-1, keepdims=True)
    var = jnp.mean(jnp.square(t32 - mu), axis=-1, keepdims=True)
    out = (t32 - mu) * lax.rsqrt(var + EPS) * g.astype(jnp.float32) + b.astype(jnp.float32)
    return out.astype(t.dtype)


def partial_rope(t, cos, sin):
    half = ROPE_DIM // 2
    t1 = t[..., :half]
    t2 = t[..., half:ROPE_DIM]
    return jnp.concatenate([t1 * cos - t2 * sin, t2 * cos + t1 * sin, t[..., ROPE_DIM:]], axis=-1)


def sliding_window_attention(q, k, v, sinks):
    b, s = q.shape[0], q.shape[1]
    nb = s // BLOCK
    qb = q.reshape(b, nb, BLOCK, N_KV_HEADS, GQA_GROUP, HEAD_DIM)

    def band(t):
        tb = t.reshape(b, nb, BLOCK, N_KV_HEADS, HEAD_DIM)
        prev = jnp.pad(tb, ((0, 0), (1, 0), (0, 0), (0, 0), (0, 0)))[:, :-1]
        return jnp.concatenate([prev, tb], axis=2)

    kb, vb = band(k), band(v)
    scores = jnp.einsum('bnqkgd,bnjkd->bnkgqj', qb, kb) * (HEAD_DIM ** -0.5)
    qi = jnp.arange(BLOCK)[:, None]
    kj = jnp.arange(2 * BLOCK)[None, :]
    dist = qi + BLOCK - kj
    in_window = (dist >= 0) & (dist < WINDOW)
    has_prev = (jnp.arange(nb)[:, None, None] > 0) | (kj[None] >= BLOCK)
    mask = in_window[None] & has_prev
    scores = jnp.where(mask[None, :, None, None], scores, NEG_INF)
    sink = sinks.astype(jnp.float32).reshape(N_KV_HEADS, GQA_GROUP)[None, None, :, :, None, None]
    m = jnp.maximum(jnp.max(scores, axis=-1, keepdims=True), sink)
    pr = jnp.exp(scores - m)
    denom = jnp.sum(pr, axis=-1, keepdims=True) + jnp.exp(sink - m)
    out = jnp.einsum('bnkgqj,bnjkd->bnqkgd', pr / denom, vb)
    return out.reshape(b, s, Q_WIDTH)


def s5_ssm(u, lam_re, lam_im, log_dt, b_re, b_im, c_re, c_im, d_skip):
    b, s, _ = u.shape
    ug = u.reshape(b, s, SSM_GROUPS, SSM_GROUP)
    lr = jnp.minimum(lam_re.astype(jnp.float32), -1e-4)
    li = lam_im.astype(jnp.float32)
    dt = jnp.exp(log_dt.astype(jnp.float32))[:, None]
    mag = jnp.exp(lr * dt)
    a_re = mag * jnp.cos(li * dt)
    a_im = mag * jnp.sin(li * dt)
    den = lr * lr + li * li
    x_re, x_im = a_re - 1.0, a_im
    f_re = (x_re * lr + x_im * li) / den
    f_im = (x_im * lr - x_re * li) / den
    br = b_re.astype(jnp.float32)
    bi = b_im.astype(jnp.float32)
    bb_re = f_re[..., None] * br - f_im[..., None] * bi
    bb_im = f_re[..., None] * bi + f_im[..., None] * br
    bu_re = jnp.einsum('bsgh,gnh->bsgn', ug, bb_re)
    bu_im = jnp.einsum('bsgh,gnh->bsgn', ug, bb_im)
    a_re_t = jnp.broadcast_to(a_re, (1, s) + a_re.shape)
    a_im_t = jnp.broadcast_to(a_im, (1, s) + a_im.shape)

    def combine(e1, e2):
        a1r, a1i, b1r, b1i = e1
        a2r, a2i, b2r, b2i = e2
        return (a2r * a1r - a2i * a1i, a2r * a1i + a2i * a1r,
                a2r * b1r - a2i * b1i + b2r, a2r * b1i + a2i * b1r + b2i)

    _, _, st_re, st_im = lax.associative_scan(combine, (a_re_t, a_im_t, bu_re, bu_im), axis=1)
    y = (jnp.einsum('bsgn,ghn->bsgh', st_re, c_re.astype(jnp.float32))
         - jnp.einsum('bsgn,ghn->bsgh', st_im, c_im.astype(jnp.float32)))
    return y.reshape(b, s, SSM_WIDTH) + d_skip.astype(jnp.float32) * u


def conformer_conv(c_in, dw_w, dw_b, ln_g, ln_b, w_pw_out):
    a, g = jnp.split(c_in, 2, axis=-1)
    u = a * jax.nn.sigmoid(g)
    u = lax.conv_general_dilated(u, dw_w[:, None, :].astype(u.dtype), window_strides=(1,),
                                 padding=[(CONV_K - 1, 0)],
                                 dimension_numbers=('NWC', 'WIO', 'NWC'),
                                 feature_group_count=CONV_WIDTH) + dw_b
    u = jax.nn.silu(layer_norm(u, ln_g, ln_b))
    return u @ w_pw_out


def setup_inputs(seed: int = 0) -> dict:
    key = jax.random.key(seed)
    ks = iter(jax.random.split(key, 40))
    f32 = jnp.float32

    def nrm(shape, scale):
        return jax.random.normal(next(ks), shape, f32) * scale

    L, D = DEPTH, D_MODEL
    x = nrm((BATCH, SEQ, D), 1.0)
    p = nrm((DEPTH, BATCH, SEQ, PLE_DIM), 1.0)
    positions = (jnp.arange(SEQ, dtype=jnp.int32)[None, :]
                 + jax.random.randint(next(ks), (BATCH, 1), 0, 1024, dtype=jnp.int32))
    n_idx = jnp.arange(SSM_STATE, dtype=f32)
    return {
        'x': x,
        'p': p,
        'positions': positions,
        'mix_norm_g': 1.0 + nrm((L, D), 0.02),
        'w_in': nrm((L, D, IN_WIDTH), D ** -0.5),
        'b_gate': nrm((L, N_BRANCH * D), 0.02),
        'attn_sinks': nrm((L, N_Q_HEADS), 0.5),
        'w_attn_out': nrm((L, Q_WIDTH, D), Q_WIDTH ** -0.5),
        'ssm_lambda_re': -0.5 + nrm((L, SSM_GROUPS, SSM_STATE), 0.01),
        'ssm_lambda_im': math.pi * n_idx + nrm((L, SSM_GROUPS, SSM_STATE), 0.01),
        'ssm_log_dt': jax.random.uniform(next(ks), (L, SSM_GROUPS), f32, math.log(1e-3), math.log(1e-1)),
        'ssm_b_re': nrm((L, SSM_GROUPS, SSM_STATE, SSM_GROUP), (2 * SSM_GROUP) ** -0.5),
        'ssm_b_im': nrm((L, SSM_GROUPS, SSM_STATE, SSM_GROUP), (2 * SSM_GROUP) ** -0.5),
        'ssm_c_re': nrm((L, SSM_GROUPS, SSM_GROUP, SSM_STATE), (2 * SSM_STATE) ** -0.5),
        'ssm_c_im': nrm((L, SSM_GROUPS, SSM_GROUP, SSM_STATE), (2 * SSM_STATE) ** -0.5),
        'ssm_d': nrm((L, SSM_WIDTH), 1.0),
        'w_ssm_glu': nrm((L, SSM_WIDTH, 2 * D), SSM_WIDTH ** -0.5),
        'b_ssm_glu': nrm((L, 2 * D), 0.02),
        'conv_dw_w': nrm((L, CONV_K, CONV_WIDTH), CONV_K ** -0.5),
        'conv_dw_b': nrm((L, CONV_WIDTH), 0.02),
        'conv_norm_g': 1.0 + nrm((L, CONV_WIDTH), 0.02),
        'conv_norm_b': nrm((L, CONV_WIDTH), 0.02),
        'w_conv_out': nrm((L, CONV_WIDTH, D), CONV_WIDTH ** -0.5),
        'w_mix_out': nrm((L, D, D), D ** -0.5),
        'ffn_norm_g': 1.0 + nrm((L, D), 0.02),
        'w_ffn_in': nrm((L, D, 2 * FFN_HIDDEN), D ** -0.5),
        'w_ffn_out': nrm((L, FFN_HIDDEN, D), FFN_HIDDEN ** -0.5),
        'w_ple_in': nrm((L, PLE_DIM, D), PLE_DIM ** -0.5),
        'ple_norm_g': 1.0 + nrm((L, D), 0.02),
        'w_ple_gate': nrm((L, D, D), D ** -0.5),
        'final_norm_g': 1.0 + nrm((D,), 0.02),
    }


def reference(x, p, positions, mix_norm_g, w_in, b_gate, attn_sinks, w_attn_out,
              ssm_lambda_re, ssm_lambda_im, ssm_log_dt, ssm_b_re, ssm_b_im, ssm_c_re, ssm_c_im,
              ssm_d, w_ssm_glu, b_ssm_glu, conv_dw_w, conv_dw_b, conv_norm_g, conv_norm_b,
              w_conv_out, w_mix_out, ffn_norm_g, w_ffn_in, w_ffn_out, w_ple_in, ple_norm_g,
              w_ple_gate, final_norm_g):
    f32 = jnp.float32
    b, s, d = x.shape
    inv_freq = ROPE_THETA ** (-jnp.arange(0, ROPE_DIM, 2, dtype=f32) / ROPE_DIM)
    ang = positions.astype(f32)[..., None] * inv_freq
    cos = jnp.cos(ang)[:, :, None, :]
    sin = jnp.sin(ang)[:, :, None, :]

    for i in range(DEPTH):
        h = rms_norm(x, mix_norm_g[i])
        z = h @ w_in[i]
        q, k, v, s_in, c_in, g_in = jnp.split(z, SPLIT_POINTS, axis=-1)

        qh = partial_rope(q.astype(f32).reshape(b, s, N_Q_HEADS, HEAD_DIM), cos, sin)
        kh = partial_rope(k.astype(f32).reshape(b, s, N_KV_HEADS, HEAD_DIM), cos, sin)
        vh = v.astype(f32).reshape(b, s, N_KV_HEADS, HEAD_DIM)
        y_attn = sliding_window_attention(qh, kh, vh, attn_sinks[i]).astype(x.dtype) @ w_attn_out[i]

        y_s = s5_ssm(s_in.astype(f32), ssm_lambda_re[i], ssm_lambda_im[i], ssm_log_dt[i],
                     ssm_b_re[i], ssm_b_im[i], ssm_c_re[i], ssm_c_im[i], ssm_d[i])
        glu_a, glu_b = jnp.split(jax.nn.gelu(y_s).astype(x.dtype) @ w_ssm_glu[i] + b_ssm_glu[i], 2, axis=-1)
        y_ssm = glu_a * jax.nn.sigmoid(glu_b)

        y_conv = conformer_conv(c_in, conv_dw_w[i], conv_dw_b[i], conv_norm_g[i], conv_norm_b[i], w_conv_out[i])

        gates = jax.nn.sigmoid(g_in + b_gate[i]).reshape(b, s, N_BRANCH, d)
        merged = gates[:, :, 0] * y_attn + gates[:, :, 1] * y_ssm + gates[:, :, 2] * y_conv
        x = x + merged @ w_mix_out[i]

        hf = rms_norm(x, ffn_norm_g[i])
        f_gate, f_up = jnp.split(hf @ w_ffn_in[i], 2, axis=-1)
        x = x + (jax.nn.silu(f_gate) * f_up) @ w_ffn_out[i]

        e = p[i] @ w_ple_in[i]
        g_ple = jax.nn.sigmoid(rms_norm(x, ple_norm_g[i]) @ w_ple_gate[i])
        x = x + g_ple * e

    return rms_norm(x, final_norm_g)
```

```python
import functools
import math

import jax
import jax.numpy as jnp
from jax import lax
from jax.experimental import pallas as pl
from jax.experimental.pallas import tpu as pltpu

EPS = 1e-6
NEG_INF = -1e30
HEAD_DIM = 64
N_Q_HEADS = 8
N_KV_HEADS = 2
ATT_BLOCK = 128
ROPE_DIM = HEAD_DIM // 4
ROPE_THETA = 500000.0
SSM_GROUP = 16
SSM_STATE = 64
CONV_K = 31
CONV_HIST = 32
LANES = 128
SUBLANES = 8
VMEM_LIMIT_BYTES = 56 * 1024 * 1024

MIXER_TILE = 256
FFN_TILE = 512

BF16 = jnp.bfloat16
F32 = jnp.float32


def _sigmoid(t):
    return 0.5 * jnp.tanh(0.5 * t) + 0.5


def _rms(t, g):
    return t * lax.rsqrt(jnp.mean(t * t, axis=-1, keepdims=True) + EPS) * g


def _dot(a, b):
    return jnp.dot(a, b, preferred_element_type=F32)


def _dot_nt(a, b):
    return lax.dot_general(a, b, (((1,), (1,)), ((), ())), preferred_element_type=F32)


def _mixer_kernel(x_ref, cos_ref, sin_ref, g_ref, win_ref, bgate_ref, sink_ref, wao_ref,
                  bb_ref, stab_ref, cc_ref, dskip_ref, wglu_ref, bglu_ref,
                  dww_ref, dwb_ref, lng_ref, lnb_ref, wco_ref, wmix_ref,
                  o_ref,
                  kprev_ref, vprev_ref, cre_ref, cim_ref, cbuf_ref, ya_ref,
                  *, ts, d, qw, kvw, sw, cw, ns):
    s = pl.program_id(1)
    nb = ts // ATT_BLOCK
    ng = ts // SUBLANES
    o_q, o_kv, o_s, o_c, o_g = 0, qw, qw + 2 * kvw, qw + 2 * kvw + sw, qw + 2 * kvw + sw + 2 * cw

    @pl.when(s == 0)
    def _():
        kprev_ref[...] = jnp.zeros_like(kprev_ref)
        vprev_ref[...] = jnp.zeros_like(vprev_ref)
        cre_ref[...] = jnp.zeros_like(cre_ref)
        cim_ref[...] = jnp.zeros_like(cim_ref)
        cbuf_ref[0:CONV_HIST, :] = jnp.zeros((CONV_HIST, cw), F32)

    x = x_ref[...]
    h = _rms(x, g_ref[...]).astype(BF16)

    cos_t = cos_ref[...]
    sin_t = sin_ref[...]
    lane = lax.broadcasted_iota(jnp.int32, (ts, LANES), 1)
    lo8 = (lane & (HEAD_DIM - 1)) < (ROPE_DIM // 2)
    lo64 = lane < HEAD_DIM

    def rope(t):
        rot = jnp.where(lo8, pltpu.roll(t, LANES - ROPE_DIM // 2, 1), pltpu.roll(t, ROPE_DIM // 2, 1))
        return t * cos_t + rot * sin_t

    q = _dot(h, win_ref[:, o_q:o_q + qw])
    scale = HEAD_DIM ** -0.5
    qb = [(rope(q[:, c * LANES:(c + 1) * LANES]) * scale).astype(BF16) for c in range(qw // LANES)]
    kv = _dot(h, win_ref[:, o_kv:o_kv + 2 * kvw])
    k = rope(kv[:, 0:kvw])
    v = kv[:, kvw:2 * kvw]

    def variants(t):
        tr = pltpu.roll(t, HEAD_DIM, 1)
        zero = jnp.zeros_like(t)
        return [jnp.where(lo64, t, zero).astype(BF16), jnp.where(lo64, zero, tr).astype(BF16),
                jnp.where(lo64, tr, zero).astype(BF16), jnp.where(lo64, zero, t).astype(BF16)]

    kvar = variants(k)
    vvar = variants(v)

    row = lax.broadcasted_iota(jnp.int32, (2 * ATT_BLOCK, 2 * ATT_BLOCK), 0) & (ATT_BLOCK - 1)
    col = lax.broadcasted_iota(jnp.int32, (2 * ATT_BLOCK, 2 * ATT_BLOCK), 1)
    valid_cur = (col >= ATT_BLOCK) & ((col - ATT_BLOCK) <= row)
    valid_prev = (col < ATT_BLOCK) & (col > row)
    first_lim = jnp.where(s == 0, ATT_BLOCK, 0)
    valid_mid = valid_cur | valid_prev
    valid_first = valid_cur | (valid_prev & (col >= first_lim))
    lo64_2 = lax.broadcasted_iota(jnp.int32, (2 * ATT_BLOCK, LANES), 1) < HEAD_DIM

    for j in range(nb):
        r0, r1 = j * ATT_BLOCK, (j + 1) * ATT_BLOCK
        valid = valid_first if j == 0 else valid_mid

        def keys(var, prev_ref, idx):
            prev = prev_ref[idx] if j == 0 else var[idx][r0 - ATT_BLOCK:r0]
            return jnp.concatenate([prev, var[idx][r0:r1]], axis=0)

        for kh in range(N_KV_HEADS):
            qs = jnp.concatenate([qb[2 * kh][r0:r1], qb[2 * kh + 1][r0:r1]], axis=0)
            probs, inv = [], []
            for par in range(2):
                sc = _dot_nt(qs, keys(kvar, kprev_ref, 2 * kh + par))
                sc = jnp.where(valid, sc, NEG_INF)
                sink = sink_ref[2 * kh + par]
                m = jnp.maximum(jnp.max(sc, axis=-1, keepdims=True), sink)
                p = jnp.exp(sc - m)
                den = jnp.sum(p, axis=-1, keepdims=True) + jnp.exp(sink - m)
                probs.append(p.astype(BF16))
                inv.append(1.0 / den)
            pcat = jnp.concatenate(probs, axis=1)
            vcat = jnp.concatenate([keys(vvar, vprev_ref, 2 * kh), keys(vvar, vprev_ref, 2 * kh + 1)], axis=0)
            o = _dot(pcat, vcat) * jnp.where(lo64_2, inv[0], inv[1])
            ya_ref[r0:r1, (2 * kh) * LANES:(2 * kh + 1) * LANES] = o[0:ATT_BLOCK].astype(BF16)
            ya_ref[r0:r1, (2 * kh + 1) * LANES:(2 * kh + 2) * LANES] = o[ATT_BLOCK:].astype(BF16)

    for idx in range(4):
        kprev_ref[idx] = kvar[idx][ts - ATT_BLOCK:ts]
        vprev_ref[idx] = vvar[idx][ts - ATT_BLOCK:ts]

    y_attn = _dot(ya_ref[...], wao_ref[...])
    merged = _sigmoid(_dot(h, win_ref[:, o_g:o_g + d]) + bgate_ref[:, 0:d]) * y_attn

    u = _dot(h, win_ref[:, o_s:o_s + sw])
    bu = _dot(u.astype(BF16), bb_ref[...])
    xr = bu[:, 0:ns].reshape(ng, SUBLANES, ns)
    xi = bu[:, ns:2 * ns].reshape(ng, SUBLANES, ns)
    for lvl, sh in enumerate((1, 2, 4)):
        ar = stab_ref[2 * lvl]
        ai = stab_ref[2 * lvl + 1]
        sr = pltpu.roll(xr, sh, 1)
        si = pltpu.roll(xi, sh, 1)
        xr, xi = xr + (ar * sr - ai * si), xi + (ar * si + ai * sr)
    pr = stab_ref[6]
    pi = stab_ref[7]
    cr = cre_ref[...]
    ci = cim_ref[...]
    rows_r, rows_i = [], []
    for i in range(ng):
        gr = xr[i] + (pr * cr - pi * ci)
        gi = xi[i] + (pr * ci + pi * cr)
        rows_r.append(gr)
        rows_i.append(gi)
        cr = jnp.broadcast_to(gr[SUBLANES - 1:SUBLANES, :], (SUBLANES, ns))
        ci = jnp.broadcast_to(gi[SUBLANES - 1:SUBLANES, :], (SUBLANES, ns))
    cre_ref[...] = cr
    cim_ref[...] = ci
    st = jnp.concatenate([jnp.concatenate(rows_r, axis=0), jnp.concatenate(rows_i, axis=0)], axis=1)
    y_s = _dot(st.astype(BF16), cc_ref[...]) + dskip_ref[...] * u
    glu = _dot(jax.nn.gelu(y_s).astype(BF16), wglu_ref[...]) + bglu_ref[...]
    y_ssm = glu[:, 0:d] * _sigmoid(glu[:, d:2 * d])
    merged = merged + _sigmoid(_dot(h, win_ref[:, o_g + d:o_g + 2 * d]) + bgate_ref[:, d:2 * d]) * y_ssm

    c_in = _dot(h, win_ref[:, o_c:o_c + 2 * cw])
    cu = c_in[:, 0:cw] * _sigmoid(c_in[:, cw:2 * cw])
    cbuf_ref[CONV_HIST:CONV_HIST + ts, :] = cu
    acc = jnp.zeros((ts, cw), F32) + dwb_ref[...]
    for j in range(CONV_K):
        acc = acc + dww_ref[j:j + 1, :] * cbuf_ref[CONV_HIST - (CONV_K - 1) + j:CONV_HIST - (CONV_K - 1) + j + ts, :]
    cbuf_ref[0:CONV_HIST, :] = cbuf_ref[ts:ts + CONV_HIST, :]
    mu = jnp.mean(acc, axis=-1, keepdims=True)
    cen = acc - mu
    var = jnp.mean(cen * cen, axis=-1, keepdims=True)
    ln = cen * lax.rsqrt(var + EPS) * lng_ref[...] + lnb_ref[...]
    y_conv = _dot((ln * _sigmoid(ln)).astype(BF16), wco_ref[...])
    merged = merged + _sigmoid(_dot(h, win_ref[:, o_g + 2 * d:o_g + 3 * d]) + bgate_ref[:, 2 * d:3 * d]) * y_conv

    o_ref[...] = x + _dot(merged.astype(BF16), wmix_ref[...])


def _const_spec(shape, layer=None):
    if layer is None:
        return pl.BlockSpec(shape, lambda *_: (0,) * len(shape), pipeline_mode=pl.Buffered(1))
    return pl.BlockSpec((None,) + shape, lambda *_: (layer,) + (0,) * len(shape), pipeline_mode=pl.Buffered(1))


def _mixer_call(x, cos_t, sin_t, layer, prm, ts):
    b, s, d = x.shape
    qw = N_Q_HEADS * HEAD_DIM
    kvw = N_KV_HEADS * HEAD_DIM
    sw = prm['bb'].shape[1]
    ns = prm['bb'].shape[2] // 2
    cw = prm['dww'].shape[2]
    inw = prm['w_in'].shape[2]
    assert s % ts == 0 and ts % ATT_BLOCK == 0
    assert kvw == LANES and qw % LANES == 0 and inw == qw + 2 * kvw + sw + 2 * cw + 3 * d
    kern = functools.partial(_mixer_kernel, ts=ts, d=d, qw=qw, kvw=kvw, sw=sw, cw=cw, ns=ns)
    row_spec = lambda w: pl.BlockSpec((None, ts, w), lambda bi, si: (bi, si, 0))
    in_specs = [
        row_spec(d), row_spec(LANES), row_spec(LANES),
        _const_spec((1, d), layer),
        _const_spec((d, inw), layer),
        _const_spec((1, 3 * d), layer),
        _const_spec((4, 2 * ATT_BLOCK, 1), layer),
        _const_spec((qw, d), layer),
        _const_spec((sw, 2 * ns), layer),
        _const_spec((8, SUBLANES, ns), layer),
        _const_spec((2 * ns, sw), layer),
        _const_spec((1, sw), layer),
        _const_spec((sw, 2 * d), layer),
        _const_spec((1, 2 * d), layer),
        _const_spec((CONV_K, cw), layer),
        _const_spec((1, cw), layer),
        _const_spec((1, cw), layer),
        _const_spec((1, cw), layer),
        _const_spec((cw, d), layer),
        _const_spec((d, d), layer),
    ]
    scratch = [
        pltpu.VMEM((4, ATT_BLOCK, LANES), BF16),
        pltpu.VMEM((4, ATT_BLOCK, LANES), BF16),
        pltpu.VMEM((SUBLANES, ns), F32),
        pltpu.VMEM((SUBLANES, ns), F32),
        pltpu.VMEM((CONV_HIST + ts, cw), F32),
        pltpu.VMEM((ts, qw), BF16),
    ]
    return pl.pallas_call(
        kern,
        grid=(b, s // ts),
        in_specs=in_specs,
        out_specs=pl.BlockSpec((None, ts, d), lambda bi, si: (bi, si, 0)),
        out_shape=jax.ShapeDtypeStruct((b, s, d), F32),
        scratch_shapes=scratch,
        compiler_params=pltpu.CompilerParams(
            dimension_semantics=("arbitrary", "arbitrary"), vmem_limit_bytes=VMEM_LIMIT_BYTES),
        name=f"mixer_l{layer}",
    )(x, cos_t, sin_t, prm['mix_g'], prm['w_in'], prm['b_gate'], prm['sinks'], prm['w_ao'],
      prm['bb'], prm['stab'], prm['cc'], prm['dskip'], prm['w_glu'], prm['b_glu'],
      prm['dww'], prm['dwb'], prm['lng'], prm['lnb'], prm['w_co'], prm['w_mix'])


def _ffn_kernel(x_ref, p_ref, gf_ref, wfi_ref, wfo_ref, wpi_ref, gp_ref, wpg_ref, gfin_ref, o_ref,
                *, hidden, final):
    x = x_ref[...]
    hf = _rms(x, gf_ref[...]).astype(BF16)
    f = _dot(hf, wfi_ref[...])
    gate = f[:, 0:hidden]
    hid = (gate * _sigmoid(gate) * f[:, hidden:2 * hidden]).astype(BF16)
    x = x + _dot(hid, wfo_ref[...])
    e = _dot(p_ref[...].astype(BF16), wpi_ref[...])
    gp = _sigmoid(_dot(_rms(x, gp_ref[...]).astype(BF16), wpg_ref[...]))
    x = x + gp * e
    if final:
        x = _rms(x, gfin_ref[...])
    o_ref[...] = x


def _ffn_call(x2, p2, layer, prm, gfin, tm, final):
    n, d = x2.shape[1], x2.shape[2]
    pd = p2.shape[2]
    hidden = prm['w_fo'].shape[1]
    assert n % tm == 0
    kern = functools.partial(_ffn_kernel, hidden=hidden, final=final)
    in_specs = [
        pl.BlockSpec((None, tm, d), lambda i: (0, i, 0)),
        pl.BlockSpec((None, tm, pd), lambda i: (layer, i, 0)),
        _const_spec((1, d), layer),
        _const_spec((d, 2 * hidden), layer),
        _const_spec((hidden, d), layer),
        _const_spec((pd, d), layer),
        _const_spec((1, d), layer),
        _const_spec((d, d), layer),
        _const_spec((1, d)),
    ]
    return pl.pallas_call(
        kern,
        grid=(n // tm,),
        in_specs=in_specs,
        out_specs=pl.BlockSpec((None, tm, d), lambda i: (0, i, 0)),
        out_shape=jax.ShapeDtypeStruct((1, n, d), F32),
        compiler_params=pltpu.CompilerParams(
            dimension_semantics=("arbitrary",), vmem_limit_bytes=VMEM_LIMIT_BYTES),
        name=f"ffn_l{layer}",
    )(x2, p2, prm['ffn_g'], prm['w_fi'], prm['w_fo'], prm['w_pi'], prm['ple_g'], prm['w_pg'], gfin)


def _rope_tables(positions):
    half = ROPE_DIM // 2
    inv_freq = ROPE_THETA ** (-jnp.arange(0, ROPE_DIM, 2, dtype=F32) / ROPE_DIM)
    ang = positions.astype(F32)[..., None] * inv_freq
    cos, sin = jnp.cos(ang), jnp.sin(ang)
    ones = jnp.ones(ang.shape[:-1] + (HEAD_DIM - ROPE_DIM,), F32)
    c64 = jnp.concatenate([cos, cos, ones], axis=-1)
    s64 = jnp.concatenate([-sin, sin, 0.0 * ones], axis=-1)
    reps = LANES // HEAD_DIM
    return jnp.concatenate([c64] * reps, axis=-1), jnp.concatenate([s64] * reps, axis=-1)


def _ssm_tables(lam_re, lam_im, log_dt, b_re, b_im, c_re, c_im):
    nl, g, n = lam_re.shape
    hh = b_re.shape[-1]
    lr = jnp.minimum(lam_re.astype(F32), -1e-4)
    li = lam_im.astype(F32)
    dt = jnp.exp(log_dt.astype(F32))[..., None]
    mag = jnp.exp(lr * dt)
    a_re = mag * jnp.cos(li * dt)
    a_im = mag * jnp.sin(li * dt)
    den = lr * lr + li * li
    x_re, x_im = a_re - 1.0, a_im
    f_re = (x_re * lr + x_im * li) / den
    f_im = (x_im * lr - x_re * li) / den
    br, bi = b_re.astype(F32), b_im.astype(F32)
    bb_re = f_re[..., None] * br - f_im[..., None] * bi
    bb_im = f_re[..., None] * bi + f_im[..., None] * br
    eye = jnp.eye(g, dtype=F32)
    blk_b = lambda t: jnp.einsum('lgnh,gk->lghkn', t, eye).reshape(nl, g * hh, g * n)
    bb = jnp.concatenate([blk_b(bb_re), blk_b(bb_im)], axis=2).astype(BF16)
    blk_c = lambda t: jnp.einsum('lghn,gk->lgnkh', t, eye).reshape(nl, g * n, g * hh)
    cc = jnp.concatenate([blk_c(c_re.astype(F32)), -blk_c(c_im.astype(F32))], axis=1).astype(BF16)

    def a_pow(kk):
        kk = kk.astype(F32)[None, :, None, None]
        m = jnp.exp(kk * (lr * dt)[:, None])
        ph = kk * (li * dt)[:, None]
        return (m * jnp.cos(ph)).reshape(nl, -1, g * n), (m * jnp.sin(ph)).reshape(nl, -1, g * n)

    rows = jnp.arange(SUBLANES)
    tabs = []
    for sh in (1, 2, 4):
        pr_, pi_ = a_pow(jnp.full((SUBLANES,), sh))
        keep = (rows >= sh).astype(F32)[None, :, None]
        tabs += [pr_ * keep, pi_ * keep]
    pr_, pi_ = a_pow(rows + 1)
    tabs += [pr_, pi_]
    return bb, cc, jnp.stack(tabs, axis=1)


def _sink_rows(attn_sinks):
    group = N_Q_HEADS // N_KV_HEADS
    out = []
    for kh in range(N_KV_HEADS):
        for par in range(2):
            a = attn_sinks[:, group * kh + par]
            c = attn_sinks[:, group * kh + 2 + par]
            col = jnp.concatenate([jnp.broadcast_to(a[:, None], (a.shape[0], ATT_BLOCK)),
                                   jnp.broadcast_to(c[:, None], (a.shape[0], ATT_BLOCK))], axis=1)
            out.append(col)
    return jnp.stack(out, axis=1)[..., None].astype(F32)


def kernel(x, p, positions, mix_norm_g, w_in, b_gate, attn_sinks, w_attn_out, ssm_lambda_re, ssm_lambda_im, ssm_log_dt, ssm_b_re, ssm_b_im, ssm_c_re, ssm_c_im, ssm_d, w_ssm_glu, b_ssm_glu, conv_dw_w, conv_dw_b, conv_norm_g, conv_norm_b, w_conv_out, w_mix_out, ffn_norm_g, w_ffn_in, w_ffn_out, w_ple_in, ple_norm_g, w_ple_gate, final_norm_g):
    b, s, d = x.shape
    depth = w_in.shape[0]
    ts = min(MIXER_TILE, s)
    tm = min(FFN_TILE, b * s)
    cos_t, sin_t = _rope_tables(positions)
    bb, cc, stab = _ssm_tables(ssm_lambda_re, ssm_lambda_im, ssm_log_dt, ssm_b_re, ssm_b_im, ssm_c_re, ssm_c_im)
    row = lambda t: t.astype(F32)[:, None, :]
    mix = dict(
        mix_g=row(mix_norm_g), w_in=w_in.astype(BF16), b_gate=row(b_gate), sinks=_sink_rows(attn_sinks),
        w_ao=w_attn_out.astype(BF16), bb=bb, stab=stab, cc=cc, dskip=row(ssm_d),
        w_glu=w_ssm_glu.astype(BF16), b_glu=row(b_ssm_glu), dww=conv_dw_w.astype(F32), dwb=row(conv_dw_b),
        lng=row(conv_norm_g), lnb=row(conv_norm_b), w_co=w_conv_out.astype(BF16), w_mix=w_mix_out.astype(BF16))
    ffn = dict(
        ffn_g=row(ffn_norm_g), w_fi=w_ffn_in.astype(BF16), w_fo=w_ffn_out.astype(BF16),
        w_pi=w_ple_in.astype(BF16), ple_g=row(ple_norm_g), w_pg=w_ple_gate.astype(BF16))
    gfin = final_norm_g.astype(F32)[None, :]
    p2 = p.reshape(depth, b * s, p.shape[-1])
    for i in range(depth):
        x = _mixer_call(x, cos_t, sin_t, i, mix, ts)
        x = _ffn_call(x.reshape(1, b * s, d), p2, i, ffn, gfin, tm, final=(i == depth - 1)).reshape(b, s, d)
    return x
```

```python
import functools

import numpy as np
import jax
import jax.numpy as jnp
from jax import lax
from jax.experimental import pallas as pl
from jax.experimental.pallas import tpu as pltpu

EPS = 1e-6
NEG_INF = -1e30
HEAD_DIM = 64
N_Q_HEADS = 8
N_KV_HEADS = 2
ATT_BLOCK = 128
ROPE_DIM = HEAD_DIM // 4
ROPE_THETA = 500000.0
CONV_K = 31
CONV_HIST = 32
LANES = 128
SUBLANES = 8
VMEM_LIMIT_BYTES = 56 * 1024 * 1024

MIXER_TILE = 256
FFN_TILE = 512
GATE_CHUNK = 256

BF16 = jnp.bfloat16
F32 = jnp.float32


def _sigmoid(t):
    return 0.5 * jnp.tanh(0.5 * t) + 0.5


def _rms(t, g):
    return t * lax.rsqrt(jnp.mean(t * t, axis=-1, keepdims=True) + EPS) * g


_dot = functools.partial(jnp.dot, preferred_element_type=F32)
_dot_nt = functools.partial(lax.dot_general, dimension_numbers=(((1,), (1,)), ((), ())),
                            preferred_element_type=F32)


def _mixer_kernel(x_ref, cos_ref, sin_ref, g_ref, win_ref, bgate_ref, sink_ref, wao_ref,
                  bb_ref, stab_ref, cc_ref, dskip_ref, wglu_ref, bglu_ref,
                  dww_ref, dwb_ref, lng_ref, lnb_ref, wco_ref, wmix_ref,
                  o_ref,
                  kprev_ref, vprev_ref, cre_ref, cim_ref, cbuf_ref, ya_ref, zg_ref, st_ref,
                  *, ts, d, qw, kvw, sw, cw, ns):
    s = pl.program_id(1)
    nb = ts // ATT_BLOCK
    o_q, o_kv, o_s, o_c, o_g = 0, qw, qw + 2 * kvw, qw + 2 * kvw + sw, qw + 2 * kvw + sw + 2 * cw

    @pl.when(s == 0)
    def _():
        kprev_ref[...] = jnp.zeros_like(kprev_ref)
        vprev_ref[...] = jnp.zeros_like(vprev_ref)
        cre_ref[...] = jnp.zeros_like(cre_ref)
        cim_ref[...] = jnp.zeros_like(cim_ref)
        cbuf_ref[0:CONV_HIST, :] = jnp.zeros((CONV_HIST, cw), F32)

    x = x_ref[...]
    h = _rms(x, g_ref[...]).astype(BF16)

    q = _dot(h, win_ref[:, o_q:o_q + qw])
    kv = _dot(h, win_ref[:, o_kv:o_kv + 2 * kvw])
    u = _dot(h, win_ref[:, o_s:o_s + sw])
    ub = u.astype(BF16)

    cos_t = cos_ref[...]
    sin_t = sin_ref[...]
    lane = lax.broadcasted_iota(jnp.int32, (ts, LANES), 1)
    lo8 = (lane & (HEAD_DIM - 1)) < (ROPE_DIM // 2)
    lo64 = lane < HEAD_DIM

    def rope(t):
        rot = jnp.where(lo8, pltpu.roll(t, LANES - ROPE_DIM // 2, 1), pltpu.roll(t, ROPE_DIM // 2, 1))
        return t * cos_t + rot * sin_t

    scale = HEAD_DIM ** -0.5
    qb = [(rope(q[:, c * LANES:(c + 1) * LANES]) * scale).astype(BF16) for c in range(qw // LANES)]
    k = rope(kv[:, 0:kvw])
    v = kv[:, kvw:2 * kvw]

    def variants(t):
        tr = pltpu.roll(t, HEAD_DIM, 1)
        zero = jnp.zeros_like(t)
        return [jnp.where(lo64, t, zero).astype(BF16), jnp.where(lo64, zero, tr).astype(BF16),
                jnp.where(lo64, tr, zero).astype(BF16), jnp.where(lo64, zero, t).astype(BF16)]

    kvar = variants(k)
    vvar = variants(v)

    def gate_chunk(i):
        zg_ref[:, i * GATE_CHUNK:(i + 1) * GATE_CHUNK] = _dot(
            h, win_ref[:, o_g + i * GATE_CHUNK:o_g + (i + 1) * GATE_CHUNK])

    blk2 = 2 * ATT_BLOCK
    row = lax.broadcasted_iota(jnp.int32, (blk2, 2 * blk2), 0) & (ATT_BLOCK - 1)
    col = lax.broadcasted_iota(jnp.int32, (blk2, 2 * blk2), 1) & (blk2 - 1)
    valid_cur = (col >= ATT_BLOCK) & ((col - ATT_BLOCK) <= row)
    valid_prev = (col < ATT_BLOCK) & (col > row)
    first_lim = jnp.where(s == 0, ATT_BLOCK, 0)
    valid_mid = valid_cur | valid_prev
    valid_first = valid_cur | (valid_prev & (col >= first_lim))
    lo64_2 = lax.broadcasted_iota(jnp.int32, (blk2, LANES), 1) < HEAD_DIM
    att = {}

    def keys(var, prev_ref, idx, j):
        r0 = j * ATT_BLOCK
        prev = prev_ref[idx] if j == 0 else var[idx][r0 - ATT_BLOCK:r0]
        return [prev, var[idx][r0:r0 + ATT_BLOCK]]

    def att_qk(unit):
        j, kh = divmod(unit, N_KV_HEADS)
        r0 = j * ATT_BLOCK
        qs = jnp.concatenate([qb[2 * kh][r0:r0 + ATT_BLOCK], qb[2 * kh + 1][r0:r0 + ATT_BLOCK]], axis=0)
        kcat = jnp.concatenate(keys(kvar, kprev_ref, 2 * kh, j) + keys(kvar, kprev_ref, 2 * kh + 1, j), axis=0)
        att[unit] = _dot_nt(qs, kcat)

    def att_softmax(unit):
        j, kh = divmod(unit, N_KV_HEADS)
        sc = jnp.where(valid_first if j == 0 else valid_mid, att[unit], NEG_INF)
        probs, inv = [], []
        for par in range(2):
            sp = sc[:, par * blk2:(par + 1) * blk2]
            sink = sink_ref[2 * kh + par]
            m = jnp.maximum(jnp.max(sp, axis=-1, keepdims=True), sink)
            p = jnp.exp(sp - m)
            den = jnp.sum(p, axis=-1, keepdims=True) + jnp.exp(sink - m)
            probs.append(p.astype(BF16))
            inv.append(1.0 / den)
        att[unit] = (jnp.concatenate(probs, axis=1), jnp.where(lo64_2, inv[0], inv[1]))

    def att_pv(unit):
        j, kh = divmod(unit, N_KV_HEADS)
        r0 = j * ATT_BLOCK
        pcat, inv = att.pop(unit)
        vcat = jnp.concatenate(keys(vvar, vprev_ref, 2 * kh, j) + keys(vvar, vprev_ref, 2 * kh + 1, j), axis=0)
        o = _dot(pcat, vcat) * inv
        ya_ref[r0:r0 + ATT_BLOCK, (2 * kh) * LANES:(2 * kh + 1) * LANES] = o[0:ATT_BLOCK].astype(BF16)
        ya_ref[r0:r0 + ATT_BLOCK, (2 * kh + 1) * LANES:(2 * kh + 2) * LANES] = o[ATT_BLOCK:].astype(BF16)

    ncol = ns // LANES
    carry = [[cre_ref[:, c * LANES:(c + 1) * LANES] for c in range(ncol)],
             [cim_ref[:, c * LANES:(c + 1) * LANES] for c in range(ncol)]]

    def scan_tile(bu_c, r0, c):
        cs = slice(c * LANES, (c + 1) * LANES)
        xr = bu_c[r0:r0 + SUBLANES, 0:LANES]
        xi = bu_c[r0:r0 + SUBLANES, LANES:2 * LANES]
        for lvl, sh in enumerate((1, 2, 4)):
            ar = stab_ref[2 * lvl, :, cs]
            ai = stab_ref[2 * lvl + 1, :, cs]
            sr = pltpu.roll(xr, sh, 0)
            si = pltpu.roll(xi, sh, 0)
            xr, xi = xr + (ar * sr - ai * si), xi + (ar * si + ai * sr)
        pr = stab_ref[6, :, cs]
        pi = stab_ref[7, :, cs]
        cr, ci = carry[0][c], carry[1][c]
        gr = xr + (pr * cr - pi * ci)
        gi = xi + (pr * ci + pi * cr)
        carry[0][c] = jnp.broadcast_to(gr[SUBLANES - 1:SUBLANES, :], (SUBLANES, LANES))
        carry[1][c] = jnp.broadcast_to(gi[SUBLANES - 1:SUBLANES, :], (SUBLANES, LANES))
        return gr, gi

    def scan_chain(c):
        bu_c = _dot(ub, bb_ref[:, 2 * c * LANES:2 * (c + 1) * LANES])
        for r0 in range(0, ts, 2 * SUBLANES):
            ar_, ai_ = scan_tile(bu_c, r0, c)
            br_, bi_ = scan_tile(bu_c, r0 + SUBLANES, c)
            st_ref[r0:r0 + 2 * SUBLANES, 2 * c * LANES:(2 * c + 1) * LANES] = (
                jnp.concatenate([ar_, br_], axis=0).astype(BF16))
            st_ref[r0:r0 + 2 * SUBLANES, (2 * c + 1) * LANES:(2 * c + 2) * LANES] = (
                jnp.concatenate([ai_, bi_], axis=0).astype(BF16))

    base = CONV_HIST - (CONV_K - 1)
    conv = []

    def conv_chunk(j):
        cs = slice(j * LANES, (j + 1) * LANES)
        c_in = _dot(h, win_ref[:, o_c + 2 * j * LANES:o_c + 2 * (j + 1) * LANES])
        ca = c_in[:, 0:LANES]
        cbuf_ref[CONV_HIST:CONV_HIST + ts, cs] = ca * jnp.tanh(c_in[:, LANES:2 * LANES]) + ca
        ext = cbuf_ref[:, cs]
        acc = jnp.zeros((ts, LANES), F32) + dwb_ref[:, cs]
        for r in range(SUBLANES):
            e = pltpu.roll(ext, CONV_HIST + ts - r, 0) if r else ext
            for o in range(base, base + CONV_K):
                if o % SUBLANES == r:
                    acc = acc + dww_ref[o - base:o - base + 1, cs] * e[o - r:o - r + ts]
        conv.append(acc)

    n_gate = 3 * d // GATE_CHUNK
    n_unit = nb * N_KV_HEADS
    n_conv = cw // LANES
    gate_ids = iter(range(n_gate))

    def next_gate():
        i = next(gate_ids, None)
        if i is not None:
            gate_chunk(i)

    for c in range(ncol):
        scan_chain(c)
        next_gate()
    for unit in range(n_unit):
        att_qk(unit)
        if unit % 2 == 0:
            next_gate()
    for unit in range(n_unit):
        att_softmax(unit)
    for unit in range(n_unit):
        att_pv(unit)
    for j in range(n_conv):
        conv_chunk(j)
        next_gate()
    for _ in range(n_gate):
        next_gate()

    for idx in range(4):
        kprev_ref[idx] = kvar[idx][ts - ATT_BLOCK:ts]
        vprev_ref[idx] = vvar[idx][ts - ATT_BLOCK:ts]
    cre_ref[...] = jnp.concatenate(carry[0], axis=1)
    cim_ref[...] = jnp.concatenate(carry[1], axis=1)
    cbuf_ref[0:CONV_HIST, :] = cbuf_ref[ts:ts + CONV_HIST, :]

    def gate_tanh(bidx):
        return jnp.tanh(zg_ref[:, bidx * d:(bidx + 1) * d] + bgate_ref[:, bidx * d:(bidx + 1) * d])

    y_s = _dot(st_ref[...], cc_ref[...]) + dskip_ref[...] * u
    t1 = gate_tanh(1)
    glu = _dot(jax.nn.gelu(y_s).astype(BF16), wglu_ref[...]) + bglu_ref[...]
    acc = jnp.concatenate(conv, axis=1)
    mu = jnp.mean(acc, axis=-1, keepdims=True)
    cen = acc - mu
    var = jnp.mean(cen * cen, axis=-1, keepdims=True)
    ln = cen * lax.rsqrt(var + EPS) * lng_ref[...] + lnb_ref[...]
    y_conv = _dot((ln * jnp.tanh(ln) + ln).astype(BF16), wco_ref[...])
    t2 = gate_tanh(2)
    y_attn = _dot(ya_ref[...], wao_ref[...])
    t0 = gate_tanh(0)
    ga = glu[:, 0:d]
    y_ssm = ga * jnp.tanh(glu[:, d:2 * d]) + ga
    merged = (t0 * y_attn + y_attn) + (t1 * y_ssm + y_ssm) + (t2 * y_conv + y_conv)
    o_ref[...] = x + _dot(merged.astype(BF16), wmix_ref[...])


def _const_spec(shape, layer=None):
    if layer is None:
        return pl.BlockSpec(shape, lambda *_: (0,) * len(shape), pipeline_mode=pl.Buffered(1))
    return pl.BlockSpec((None,) + shape, lambda *_: (layer,) + (0,) * len(shape), pipeline_mode=pl.Buffered(1))


def _mixer_call(x, cos_t, sin_t, layer, prm, ts):
    b, s, d = x.shape
    qw = N_Q_HEADS * HEAD_DIM
    kvw = N_KV_HEADS * HEAD_DIM
    sw = prm['bb'].shape[1]
    ns = prm['bb'].shape[2] // 2
    cw = prm['dww'].shape[2]
    inw = prm['w_in'].shape[2]
    assert s % ts == 0 and ts % ATT_BLOCK == 0
    assert kvw == LANES and qw % LANES == 0 and inw == qw + 2 * kvw + sw + 2 * cw + 3 * d
    assert (3 * d) % GATE_CHUNK == 0 and ns % LANES == 0 and cw % LANES == 0
    kern = functools.partial(_mixer_kernel, ts=ts, d=d, qw=qw, kvw=kvw, sw=sw, cw=cw, ns=ns)
    row_spec = lambda w: pl.BlockSpec((None, ts, w), lambda bi, si: (bi, si, 0))
    in_specs = [
        row_spec(d), row_spec(LANES), row_spec(LANES),
        _const_spec((1, d), layer),
        _const_spec((d, inw), layer),
        _const_spec((1, 3 * d), layer),
        _const_spec((4, 2 * ATT_BLOCK, 1), layer),
        _const_spec((qw, d), layer),
        _const_spec((sw, 2 * ns), layer),
        _const_spec((8, SUBLANES, ns), layer),
        _const_spec((2 * ns, sw), layer),
        _const_spec((1, sw), layer),
        _const_spec((sw, 2 * d), layer),
        _const_spec((1, 2 * d), layer),
        _const_spec((CONV_K, cw), layer),
        _const_spec((1, cw), layer),
        _const_spec((1, cw), layer),
        _const_spec((1, cw), layer),
        _const_spec((cw, d), layer),
        _const_spec((d, d), layer),
    ]
    scratch = [
        pltpu.VMEM((4, ATT_BLOCK, LANES), BF16),
        pltpu.VMEM((4, ATT_BLOCK, LANES), BF16),
        pltpu.VMEM((SUBLANES, ns), F32),
        pltpu.VMEM((SUBLANES, ns), F32),
        pltpu.VMEM((CONV_HIST + ts, cw), F32),
        pltpu.VMEM((ts, qw), BF16),
        pltpu.VMEM((ts, 3 * d), F32),
        pltpu.VMEM((ts, 2 * ns), BF16),
    ]
    return pl.pallas_call(
        kern,
        grid=(b, s // ts),
        in_specs=in_specs,
        out_specs=pl.BlockSpec((None, ts, d), lambda bi, si: (bi, si, 0)),
        out_shape=jax.ShapeDtypeStruct((b, s, d), F32),
        scratch_shapes=scratch,
        compiler_params=pltpu.CompilerParams(
            dimension_semantics=("arbitrary", "arbitrary"), vmem_limit_bytes=VMEM_LIMIT_BYTES),
        name=f"mixer_l{layer}",
    )(x, cos_t, sin_t, prm['mix_g'], prm['w_in'], prm['b_gate'], prm['sinks'], prm['w_ao'],
      prm['bb'], prm['stab'], prm['cc'], prm['dskip'], prm['w_glu'], prm['b_glu'],
      prm['dww'], prm['dwb'], prm['lng'], prm['lnb'], prm['w_co'], prm['w_mix'])


def _ffn_kernel(x_ref, p_ref, gf_ref, wfi_ref, wfo_ref, wpi_ref, gp_ref, wpg_ref, gfin_ref, o_ref,
                *, hidden, final):
    x = x_ref[...]
    hf = _rms(x, gf_ref[...]).astype(BF16)
    f = _dot(hf, wfi_ref[...])
    gate = f[:, 0:hidden]
    hid = (gate * _sigmoid(gate) * f[:, hidden:2 * hidden]).astype(BF16)
    x = x + _dot(hid, wfo_ref[...])
    e = _dot(p_ref[...].astype(BF16), wpi_ref[...])
    gp = _sigmoid(_dot(_rms(x, gp_ref[...]).astype(BF16), wpg_ref[...]))
    x = x + gp * e
    if final:
        x = _rms(x, gfin_ref[...])
    o_ref[...] = x


def _ffn_call(x2, p2, layer, prm, gfin, tm, final):
    n, d = x2.shape[1], x2.shape[2]
    pd = p2.shape[2]
    hidden = prm['w_fo'].shape[1]
    assert n % tm == 0
    kern = functools.partial(_ffn_kernel, hidden=hidden, final=final)
    in_specs = [
        pl.BlockSpec((None, tm, d), lambda i: (0, i, 0)),
        pl.BlockSpec((None, tm, pd), lambda i: (layer, i, 0)),
        _const_spec((1, d), layer),
        _const_spec((d, 2 * hidden), layer),
        _const_spec((hidden, d), layer),
        _const_spec((pd, d), layer),
        _const_spec((1, d), layer),
        _const_spec((d, d), layer),
        _const_spec((1, d)),
    ]
    return pl.pallas_call(
        kern,
        grid=(n // tm,),
        in_specs=in_specs,
        out_specs=pl.BlockSpec((None, tm, d), lambda i: (0, i, 0)),
        out_shape=jax.ShapeDtypeStruct((1, n, d), F32),
        compiler_params=pltpu.CompilerParams(
            dimension_semantics=("arbitrary",), vmem_limit_bytes=VMEM_LIMIT_BYTES),
        name=f"ffn_l{layer}",
    )(x2, p2, prm['ffn_g'], prm['w_fi'], prm['w_fo'], prm['w_pi'], prm['ple_g'], prm['w_pg'], gfin)


def _rope_tables(positions):
    half = ROPE_DIM // 2
    inv_freq = ROPE_THETA ** (-jnp.arange(0, ROPE_DIM, 2, dtype=F32) / ROPE_DIM)
    ang = positions.astype(F32)[..., None] * inv_freq
    lane = np.arange(LANES)
    in_rope = (lane % HEAD_DIM) < ROPE_DIM
    onehot = ((lane[None, :] % half) == np.arange(half)[:, None]) & in_rope[None, :]
    sign = np.where((lane % HEAD_DIM) < half, -1.0, 1.0)
    e_cos = jnp.asarray(onehot.astype(np.float32))
    e_sin = jnp.asarray((onehot * sign[None, :]).astype(np.float32))
    expand = lambda t, e: jnp.einsum('bsh,hl->bsl', t, e, precision=lax.Precision.HIGHEST)
    cos_t = expand(jnp.cos(ang), e_cos) + jnp.asarray((~in_rope).astype(np.float32))
    sin_t = expand(jnp.sin(ang), e_sin)
    return cos_t, sin_t


def _ssm_tables(lam_re, lam_im, log_dt, b_re, b_im, c_re, c_im):
    nl, g, n = lam_re.shape
    hh = b_re.shape[-1]
    lr = jnp.minimum(lam_re.astype(F32), -1e-4)
    li = lam_im.astype(F32)
    dt = jnp.exp(log_dt.astype(F32))[..., None]
    mag = jnp.exp(lr * dt)
    a_re = mag * jnp.cos(li * dt)
    a_im = mag * jnp.sin(li * dt)
    den = lr * lr + li * li
    x_re, x_im = a_re - 1.0, a_im
    f_re = (x_re * lr + x_im * li) / den
    f_im = (x_im * lr - x_re * li) / den
    br, bi = b_re.astype(F32), b_im.astype(F32)
    bb_re = f_re[..., None] * br - f_im[..., None] * bi
    bb_im = f_re[..., None] * bi + f_im[..., None] * br
    eye = jnp.eye(g, dtype=F32)
    blk_b = lambda t: jnp.einsum('lgnh,gk->lghkn', t, eye).reshape(nl, g * hh, g * n)
    ncol = g * n // LANES
    bb = jnp.stack([blk_b(bb_re), blk_b(bb_im)], axis=2).reshape(nl, g * hh, 2, ncol, LANES)
    bb = bb.transpose(0, 1, 3, 2, 4).reshape(nl, g * hh, 2 * g * n).astype(BF16)
    blk_c = lambda t: jnp.einsum('lghn,gk->lgnkh', t, eye).reshape(nl, g * n, g * hh)
    cc = jnp.stack([blk_c(c_re.astype(F32)), -blk_c(c_im.astype(F32))], axis=1)
    cc = cc.reshape(nl, 2, ncol, LANES, g * hh).transpose(0, 2, 1, 3, 4).reshape(nl, 2 * g * n, g * hh).astype(BF16)

    def a_pow(kk):
        kk = kk.astype(F32)[None, :, None, None]
        m = jnp.exp(kk * (lr * dt)[:, None])
        ph = kk * (li * dt)[:, None]
        return (m * jnp.cos(ph)).reshape(nl, -1, g * n), (m * jnp.sin(ph)).reshape(nl, -1, g * n)

    rows = jnp.arange(SUBLANES)
    tabs = []
    for sh in (1, 2, 4):
        pr_, pi_ = a_pow(jnp.full((SUBLANES,), sh))
        keep = (rows >= sh).astype(F32)[None, :, None]
        tabs += [pr_ * keep, pi_ * keep]
    pr_, pi_ = a_pow(rows + 1)
    tabs += [pr_, pi_]
    return bb, cc, jnp.stack(tabs, axis=1)


def _sink_rows(attn_sinks):
    group = N_Q_HEADS // N_KV_HEADS
    out = []
    for kh in range(N_KV_HEADS):
        for par in range(2):
            a = attn_sinks[:, group * kh + par]
            c = attn_sinks[:, group * kh + 2 + par]
            col = jnp.concatenate([jnp.broadcast_to(a[:, None], (a.shape[0], ATT_BLOCK)),
                                   jnp.broadcast_to(c[:, None], (a.shape[0], ATT_BLOCK))], axis=1)
            out.append(col)
    return jnp.stack(out, axis=1)[..., None].astype(F32)


def _fold_half(t, lo=None, hi=None):
    t = t.astype(F32)
    if lo is None:
        return 0.5 * t
    colscale = np.ones((t.shape[-1],), np.float32)
    colscale[lo:hi] = 0.5
    return t * jnp.asarray(colscale)


def kernel(x, p, positions, mix_norm_g, w_in, b_gate, attn_sinks, w_attn_out, ssm_lambda_re, ssm_lambda_im, ssm_log_dt, ssm_b_re, ssm_b_im, ssm_c_re, ssm_c_im, ssm_d, w_ssm_glu, b_ssm_glu, conv_dw_w, conv_dw_b, conv_norm_g, conv_norm_b, w_conv_out, w_mix_out, ffn_norm_g, w_ffn_in, w_ffn_out, w_ple_in, ple_norm_g, w_ple_gate, final_norm_g):
    b, s, d = x.shape
    depth = w_in.shape[0]
    ts = min(MIXER_TILE, s)
    tm = min(FFN_TILE, b * s)
    cos_t, sin_t = _rope_tables(positions)
    bb, cc, stab = _ssm_tables(ssm_lambda_re, ssm_lambda_im, ssm_log_dt, ssm_b_re, ssm_b_im, ssm_c_re, ssm_c_im)
    row = lambda t: t.astype(F32)[:, None, :]
    cw = conv_dw_w.shape[-1]
    o_c = w_in.shape[-1] - 3 * d - 2 * cw
    w_in_h = _fold_half(w_in, o_c, None)
    w_conv = w_in_h[..., o_c:o_c + 2 * cw].reshape(depth, d, 2, cw // LANES, LANES)
    w_conv = w_conv.transpose(0, 1, 3, 2, 4).reshape(depth, d, 2 * cw)
    w_in_h = jnp.concatenate([w_in_h[..., :o_c], w_conv, w_in_h[..., o_c + 2 * cw:]], axis=-1)
    mix = dict(
        mix_g=row(mix_norm_g), w_in=w_in_h.astype(BF16), b_gate=row(_fold_half(b_gate)),
        sinks=_sink_rows(attn_sinks), w_ao=w_attn_out.astype(BF16), bb=bb, stab=stab, cc=cc, dskip=row(ssm_d),
        w_glu=_fold_half(w_ssm_glu).astype(BF16), b_glu=row(_fold_half(b_ssm_glu)),
        dww=conv_dw_w.astype(F32), dwb=row(conv_dw_b),
        lng=row(_fold_half(conv_norm_g)), lnb=row(_fold_half(conv_norm_b)),
        w_co=w_conv_out.astype(BF16), w_mix=_fold_half(w_mix_out).astype(BF16))
    ffn = dict(
        ffn_g=row(ffn_norm_g), w_fi=w_ffn_in.astype(BF16), w_fo=w_ffn_out.astype(BF16),
        w_pi=w_ple_in.astype(BF16), ple_g=row(ple_norm_g), w_pg=w_ple_gate.astype(BF16))
    gfin = final_norm_g.astype(F32)[None, :]
    p2 = p.reshape(depth, b * s, p.shape[-1])
    for i in range(depth):
        x = _mixer_call(x, cos_t, sin_t, i, mix, ts)
        x = _ffn_call(x.reshape(1, b * s, d), p2, i, ffn, gfin, tm, final=(i == depth - 1)).reshape(b, s, d)
    return x
```

```python
import functools

import numpy as np
import jax
import jax.numpy as jnp
from jax import lax
from jax.experimental import pallas as pl
from jax.experimental.pallas import tpu as pltpu

EPS = 1e-6
NEG_INF = -1e30
HEAD_DIM = 64
N_Q_HEADS = 8
N_KV_HEADS = 2
ATT_BLOCK = 128
ROPE_DIM = HEAD_DIM // 4
ROPE_THETA = 500000.0
CONV_K = 31
CONV_HIST = 32
LANES = 128
SUBLANES = 8
VMEM_LIMIT_BYTES = 56 * 1024 * 1024

MIXER_TILE = 512
FFN_TILE = 1024
FFN_CHUNK = 1024
GATE_CHUNK = 256

BF16 = jnp.bfloat16
F32 = jnp.float32


def _sigmoid(t):
    return 0.5 * jnp.tanh(0.5 * t) + 0.5


def _rms(t, g):
    return t * lax.rsqrt(jnp.mean(t * t, axis=-1, keepdims=True) + EPS) * g


_dot = functools.partial(jnp.dot, preferred_element_type=F32)
_dot_nt = functools.partial(lax.dot_general, dimension_numbers=(((1,), (1,)), ((), ())),
                            preferred_element_type=F32)


def _mixer_kernel(x_ref, cos_ref, sin_ref, g_ref, win_ref, bgate_ref, sink_ref, wao_ref,
                  bb_ref, stab_ref, cc_ref, dskip_ref, wglu_ref, bglu_ref,
                  dww_ref, dwb_ref, lng_ref, lnb_ref, wco_ref, wmix_ref,
                  o_ref,
                  kprev_ref, vprev_ref, cre_ref, cim_ref, cbuf_ref, ya_ref, zg_ref, st_ref,
                  *, ts, d, qw, kvw, sw, cw, ns):
    s = pl.program_id(1)
    nb = ts // ATT_BLOCK
    o_q, o_kv, o_s, o_c, o_g = 0, qw, qw + 2 * kvw, qw + 2 * kvw + sw, qw + 2 * kvw + sw + 2 * cw

    @pl.when(s == 0)
    def _():
        kprev_ref[...] = jnp.zeros_like(kprev_ref)
        vprev_ref[...] = jnp.zeros_like(vprev_ref)
        cre_ref[...] = jnp.zeros_like(cre_ref)
        cim_ref[...] = jnp.zeros_like(cim_ref)
        cbuf_ref[0:CONV_HIST, :] = jnp.zeros((CONV_HIST, cw), F32)

    x = x_ref[...]
    h = _rms(x, g_ref[...]).astype(BF16)

    q = _dot(h, win_ref[:, o_q:o_q + qw])
    kv = _dot(h, win_ref[:, o_kv:o_kv + 2 * kvw])
    u = _dot(h, win_ref[:, o_s:o_s + sw])
    ub = u.astype(BF16)

    cos_t = cos_ref[...]
    sin_t = sin_ref[...]
    lane = lax.broadcasted_iota(jnp.int32, (ts, LANES), 1)
    lo8 = (lane & (HEAD_DIM - 1)) < (ROPE_DIM // 2)
    lo64 = lane < HEAD_DIM

    def rope(t):
        rot = jnp.where(lo8, pltpu.roll(t, LANES - ROPE_DIM // 2, 1), pltpu.roll(t, ROPE_DIM // 2, 1))
        return t * cos_t + rot * sin_t

    scale = HEAD_DIM ** -0.5
    qb = [(rope(q[:, c * LANES:(c + 1) * LANES]) * scale).astype(BF16) for c in range(qw // LANES)]
    k = rope(kv[:, 0:kvw])
    v = kv[:, kvw:2 * kvw]

    def variants(t):
        tr = pltpu.roll(t, HEAD_DIM, 1)
        zero = jnp.zeros_like(t)
        return [jnp.where(lo64, t, zero).astype(BF16), jnp.where(lo64, zero, tr).astype(BF16),
                jnp.where(lo64, tr, zero).astype(BF16), jnp.where(lo64, zero, t).astype(BF16)]

    kvar = variants(k)
    vvar = variants(v)

    def gate_chunk(i):
        zg_ref[:, i * GATE_CHUNK:(i + 1) * GATE_CHUNK] = _dot(
            h, win_ref[:, o_g + i * GATE_CHUNK:o_g + (i + 1) * GATE_CHUNK])

    blk2 = 2 * ATT_BLOCK
    row = lax.broadcasted_iota(jnp.int32, (blk2, 2 * blk2), 0) & (ATT_BLOCK - 1)
    col = lax.broadcasted_iota(jnp.int32, (blk2, 2 * blk2), 1) & (blk2 - 1)
    valid_cur = (col >= ATT_BLOCK) & ((col - ATT_BLOCK) <= row)
    valid_prev = (col < ATT_BLOCK) & (col > row)
    first_lim = jnp.where(s == 0, ATT_BLOCK, 0)
    valid_mid = valid_cur | valid_prev
    valid_first = valid_cur | (valid_prev & (col >= first_lim))
    lo64_2 = lax.broadcasted_iota(jnp.int32, (blk2, LANES), 1) < HEAD_DIM
    att = {}

    def keys(var, prev_ref, idx, j):
        r0 = j * ATT_BLOCK
        prev = prev_ref[idx] if j == 0 else var[idx][r0 - ATT_BLOCK:r0]
        return [prev, var[idx][r0:r0 + ATT_BLOCK]]

    def att_qk(unit):
        j, kh = divmod(unit, N_KV_HEADS)
        r0 = j * ATT_BLOCK
        qs = jnp.concatenate([qb[2 * kh][r0:r0 + ATT_BLOCK], qb[2 * kh + 1][r0:r0 + ATT_BLOCK]], axis=0)
        kcat = jnp.concatenate(keys(kvar, kprev_ref, 2 * kh, j) + keys(kvar, kprev_ref, 2 * kh + 1, j), axis=0)
        att[unit] = _dot_nt(qs, kcat)

    def att_softmax(unit):
        j, kh = divmod(unit, N_KV_HEADS)
        sc = jnp.where(valid_first if j == 0 else valid_mid, att[unit], NEG_INF)
        probs, inv = [], []
        for par in range(2):
            sp = sc[:, par * blk2:(par + 1) * blk2]
            sink = sink_ref[2 * kh + par]
            m = jnp.maximum(jnp.max(sp, axis=-1, keepdims=True), sink)
            p = jnp.exp(sp - m)
            den = jnp.sum(p, axis=-1, keepdims=True) + jnp.exp(sink - m)
            probs.append(p.astype(BF16))
            inv.append(1.0 / den)
        att[unit] = (jnp.concatenate(probs, axis=1), jnp.where(lo64_2, inv[0], inv[1]))

    def att_pv(unit):
        j, kh = divmod(unit, N_KV_HEADS)
        r0 = j * ATT_BLOCK
        pcat, inv = att.pop(unit)
        vcat = jnp.concatenate(keys(vvar, vprev_ref, 2 * kh, j) + keys(vvar, vprev_ref, 2 * kh + 1, j), axis=0)
        o = _dot(pcat, vcat) * inv
        ya_ref[r0:r0 + ATT_BLOCK, (2 * kh) * LANES:(2 * kh + 1) * LANES] = o[0:ATT_BLOCK].astype(BF16)
        ya_ref[r0:r0 + ATT_BLOCK, (2 * kh + 1) * LANES:(2 * kh + 2) * LANES] = o[ATT_BLOCK:].astype(BF16)

    ncol = ns // LANES
    carry = [[cre_ref[:, c * LANES:(c + 1) * LANES] for c in range(ncol)],
             [cim_ref[:, c * LANES:(c + 1) * LANES] for c in range(ncol)]]

    def scan_tile(bu_c, r0, c):
        cs = slice(c * LANES, (c + 1) * LANES)
        xr = bu_c[r0:r0 + SUBLANES, 0:LANES]
        xi = bu_c[r0:r0 + SUBLANES, LANES:2 * LANES]
        for lvl, sh in enumerate((1, 2, 4)):
            ar = stab_ref[2 * lvl, :, cs]
            ai = stab_ref[2 * lvl + 1, :, cs]
            sr = pltpu.roll(xr, sh, 0)
            si = pltpu.roll(xi, sh, 0)
            xr, xi = xr + (ar * sr - ai * si), xi + (ar * si + ai * sr)
        pr = stab_ref[6, :, cs]
        pi = stab_ref[7, :, cs]
        cr, ci = carry[0][c], carry[1][c]
        gr = xr + (pr * cr - pi * ci)
        gi = xi + (pr * ci + pi * cr)
        carry[0][c] = jnp.broadcast_to(gr[SUBLANES - 1:SUBLANES, :], (SUBLANES, LANES))
        carry[1][c] = jnp.broadcast_to(gi[SUBLANES - 1:SUBLANES, :], (SUBLANES, LANES))
        return gr, gi

    def scan_chain(c):
        bu_c = _dot(ub, bb_ref[:, 2 * c * LANES:2 * (c + 1) * LANES])
        for r0 in range(0, ts, 2 * SUBLANES):
            ar_, ai_ = scan_tile(bu_c, r0, c)
            br_, bi_ = scan_tile(bu_c, r0 + SUBLANES, c)
            st_ref[r0:r0 + 2 * SUBLANES, 2 * c * LANES:(2 * c + 1) * LANES] = (
                jnp.concatenate([ar_, br_], axis=0).astype(BF16))
            st_ref[r0:r0 + 2 * SUBLANES, (2 * c + 1) * LANES:(2 * c + 2) * LANES] = (
                jnp.concatenate([ai_, bi_], axis=0).astype(BF16))

    base = CONV_HIST - (CONV_K - 1)
    conv = []

    def conv_chunk(j):
        cs = slice(j * LANES, (j + 1) * LANES)
        c_in = _dot(h, win_ref[:, o_c + 2 * j * LANES:o_c + 2 * (j + 1) * LANES])
        ca = c_in[:, 0:LANES]
        cbuf_ref[CONV_HIST:CONV_HIST + ts, cs] = ca * jnp.tanh(c_in[:, LANES:2 * LANES]) + ca
        ext = cbuf_ref[:, cs]
        acc = jnp.zeros((ts, LANES), F32) + dwb_ref[:, cs]
        for r in range(SUBLANES):
            e = pltpu.roll(ext, CONV_HIST + ts - r, 0) if r else ext
            for o in range(base, base + CONV_K):
                if o % SUBLANES == r:
                    acc = acc + dww_ref[o - base:o - base + 1, cs] * e[o - r:o - r + ts]
        conv.append(acc)

    n_gate = 3 * d // GATE_CHUNK
    n_unit = nb * N_KV_HEADS
    n_conv = cw // LANES
    gate_ids = iter(range(n_gate))

    def next_gate():
        i = next(gate_ids, None)
        if i is not None:
            gate_chunk(i)

    for c in range(ncol):
        scan_chain(c)
        next_gate()
    for unit in range(n_unit):
        att_qk(unit)
        if unit % 2 == 0:
            next_gate()
    for unit in range(n_unit):
        att_softmax(unit)
    for unit in range(n_unit):
        att_pv(unit)
    for j in range(n_conv):
        conv_chunk(j)
        next_gate()
    for _ in range(n_gate):
        next_gate()

    for idx in range(4):
        kprev_ref[idx] = kvar[idx][ts - ATT_BLOCK:ts]
        vprev_ref[idx] = vvar[idx][ts - ATT_BLOCK:ts]
    cre_ref[...] = jnp.concatenate(carry[0], axis=1)
    cim_ref[...] = jnp.concatenate(carry[1], axis=1)
    cbuf_ref[0:CONV_HIST, :] = cbuf_ref[ts:ts + CONV_HIST, :]

    def gate_tanh(bidx):
        return jnp.tanh(zg_ref[:, bidx * d:(bidx + 1) * d] + bgate_ref[:, bidx * d:(bidx + 1) * d])

    y_s = _dot(st_ref[...], cc_ref[...]) + dskip_ref[...] * u
    t1 = gate_tanh(1)
    glu = _dot(jax.nn.gelu(y_s).astype(BF16), wglu_ref[...]) + bglu_ref[...]
    acc = jnp.concatenate(conv, axis=1)
    mu = jnp.mean(acc, axis=-1, keepdims=True)
    cen = acc - mu
    var = jnp.mean(cen * cen, axis=-1, keepdims=True)
    ln = cen * lax.rsqrt(var + EPS) * lng_ref[...] + lnb_ref[...]
    y_conv = _dot((ln * jnp.tanh(ln) + ln).astype(BF16), wco_ref[...])
    t2 = gate_tanh(2)
    y_attn = _dot(ya_ref[...], wao_ref[...])
    t0 = gate_tanh(0)
    ga = glu[:, 0:d]
    y_ssm = ga * jnp.tanh(glu[:, d:2 * d]) + ga
    merged = (t0 * y_attn + y_attn) + (t1 * y_ssm + y_ssm) + (t2 * y_conv + y_conv)
    o_ref[...] = x + _dot(merged.astype(BF16), wmix_ref[...])


def _const_spec(shape, layer=None):
    if layer is None:
        return pl.BlockSpec(shape, lambda *_: (0,) * len(shape), pipeline_mode=pl.Buffered(1))
    return pl.BlockSpec((None,) + shape, lambda *_: (layer,) + (0,) * len(shape), pipeline_mode=pl.Buffered(1))


def _mixer_call(x, cos_t, sin_t, layer, prm, ts):
    b, s, d = x.shape
    qw = N_Q_HEADS * HEAD_DIM
    kvw = N_KV_HEADS * HEAD_DIM
    sw = prm['bb'].shape[1]
    ns = prm['bb'].shape[2] // 2
    cw = prm['dww'].shape[2]
    inw = prm['w_in'].shape[2]
    assert s % ts == 0 and ts % ATT_BLOCK == 0
    assert kvw == LANES and qw % LANES == 0 and inw == qw + 2 * kvw + sw + 2 * cw + 3 * d
    assert (3 * d) % GATE_CHUNK == 0 and ns % LANES == 0 and cw % LANES == 0
    kern = functools.partial(_mixer_kernel, ts=ts, d=d, qw=qw, kvw=kvw, sw=sw, cw=cw, ns=ns)
    row_spec = lambda w: pl.BlockSpec((None, ts, w), lambda bi, si: (bi, si, 0))
    in_specs = [
        row_spec(d), row_spec(LANES), row_spec(LANES),
        _const_spec((1, d), layer),
        _const_spec((d, inw), layer),
        _const_spec((1, 3 * d), layer),
        _const_spec((4, 2 * ATT_BLOCK, 1), layer),
        _const_spec((qw, d), layer),
        _const_spec((sw, 2 * ns), layer),
        _const_spec((8, SUBLANES, ns), layer),
        _const_spec((2 * ns, sw), layer),
        _const_spec((1, sw), layer),
        _const_spec((sw, 2 * d), layer),
        _const_spec((1, 2 * d), layer),
        _const_spec((CONV_K, cw), layer),
        _const_spec((1, cw), layer),
        _const_spec((1, cw), layer),
        _const_spec((1, cw), layer),
        _const_spec((cw, d), layer),
        _const_spec((d, d), layer),
    ]
    scratch = [
        pltpu.VMEM((4, ATT_BLOCK, LANES), BF16),
        pltpu.VMEM((4, ATT_BLOCK, LANES), BF16),
        pltpu.VMEM((SUBLANES, ns), F32),
        pltpu.VMEM((SUBLANES, ns), F32),
        pltpu.VMEM((CONV_HIST + ts, cw), F32),
        pltpu.VMEM((ts, qw), BF16),
        pltpu.VMEM((ts, 3 * d), F32),
        pltpu.VMEM((ts, 2 * ns), BF16),
    ]
    return pl.pallas_call(
        kern,
        grid=(b, s // ts),
        in_specs=in_specs,
        out_specs=pl.BlockSpec((None, ts, d), lambda bi, si: (bi, si, 0)),
        out_shape=jax.ShapeDtypeStruct((b, s, d), F32),
        scratch_shapes=scratch,
        compiler_params=pltpu.CompilerParams(
            dimension_semantics=("arbitrary", "arbitrary"), vmem_limit_bytes=VMEM_LIMIT_BYTES),
        name=f"mixer_l{layer}",
    )(x, cos_t, sin_t, prm['mix_g'], prm['w_in'], prm['b_gate'], prm['sinks'], prm['w_ao'],
      prm['bb'], prm['stab'], prm['cc'], prm['dskip'], prm['w_glu'], prm['b_glu'],
      prm['dww'], prm['dwb'], prm['lng'], prm['lnb'], prm['w_co'], prm['w_mix'])


def _ffn_kernel(x_ref, p_ref, gf_ref, wfi_ref, wfo_ref, wpi_ref, gp_ref, wpg_ref, gfin_ref, o_ref,
                *, hidden, final):
    x = x_ref[...]
    hf = _rms(x, gf_ref[...]).astype(BF16)
    for a in range(0, hidden, FFN_CHUNK):
        b = min(a + FFN_CHUNK, hidden)
        gate = _dot(hf, wfi_ref[:, a:b])
        up = _dot(hf, wfi_ref[:, hidden + a:hidden + b])
        hid = (gate * _sigmoid(gate) * up).astype(BF16)
        x = x + _dot(hid, wfo_ref[a:b, :])
    e = _dot(p_ref[...].astype(BF16), wpi_ref[...])
    gp = _sigmoid(_dot(_rms(x, gp_ref[...]).astype(BF16), wpg_ref[...]))
    x = x + gp * e
    if final:
        x = _rms(x, gfin_ref[...])
    o_ref[...] = x


def _ffn_call(x2, p2, layer, prm, gfin, tm, final):
    n, d = x2.shape[1], x2.shape[2]
    pd = p2.shape[2]
    hidden = prm['w_fo'].shape[1]
    assert n % tm == 0
    kern = functools.partial(_ffn_kernel, hidden=hidden, final=final)
    in_specs = [
        pl.BlockSpec((None, tm, d), lambda i: (0, i, 0)),
        pl.BlockSpec((None, tm, pd), lambda i: (layer, i, 0)),
        _const_spec((1, d), layer),
        _const_spec((d, 2 * hidden), layer),
        _const_spec((hidden, d), layer),
        _const_spec((pd, d), layer),
        _const_spec((1, d), layer),
        _const_spec((d, d), layer),
        _const_spec((1, d)),
    ]
    return pl.pallas_call(
        kern,
        grid=(n // tm,),
        in_specs=in_specs,
        out_specs=pl.BlockSpec((None, tm, d), lambda i: (0, i, 0)),
        out_shape=jax.ShapeDtypeStruct((1, n, d), F32),
        compiler_params=pltpu.CompilerParams(
            dimension_semantics=("arbitrary",), vmem_limit_bytes=VMEM_LIMIT_BYTES),
        name=f"ffn_l{layer}",
    )(x2, p2, prm['ffn_g'], prm['w_fi'], prm['w_fo'], prm['w_pi'], prm['ple_g'], prm['w_pg'], gfin)


def _rope_tables(positions):
    half = ROPE_DIM // 2
    inv_freq = ROPE_THETA ** (-jnp.arange(0, ROPE_DIM, 2, dtype=F32) / ROPE_DIM)
    ang = positions.astype(F32)[..., None] * inv_freq
    lane = np.arange(LANES)
    in_rope = (lane % HEAD_DIM) < ROPE_DIM
    onehot = ((lane[None, :] % half) == np.arange(half)[:, None]) & in_rope[None, :]
    sign = np.where((lane % HEAD_DIM) < half, -1.0, 1.0)
    e_cos = jnp.asarray(onehot.astype(np.float32))
    e_sin = jnp.asarray((onehot * sign[None, :]).astype(np.float32))
    expand = lambda t, e: jnp.einsum('bsh,hl->bsl', t, e, precision=lax.Precision.HIGHEST)
    cos_t = expand(jnp.cos(ang), e_cos) + jnp.asarray((~in_rope).astype(np.float32))
    sin_t = expand(jnp.sin(ang), e_sin)
    return cos_t, sin_t


def _ssm_tables(lam_re, lam_im, log_dt, b_re, b_im, c_re, c_im):
    nl, g, n = lam_re.shape
    hh = b_re.shape[-1]
    lr = jnp.minimum(lam_re.astype(F32), -1e-4)
    li = lam_im.astype(F32)
    dt = jnp.exp(log_dt.astype(F32))[..., None]
    mag = jnp.exp(lr * dt)
    a_re = mag * jnp.cos(li * dt)
    a_im = mag * jnp.sin(li * dt)
    den = lr * lr + li * li
    x_re, x_im = a_re - 1.0, a_im
    f_re = (x_re * lr + x_im * li) / den
    f_im = (x_im * lr - x_re * li) / den
    br, bi = b_re.astype(F32), b_im.astype(F32)
    bb_re = f_re[..., None] * br - f_im[..., None] * bi
    bb_im = f_re[..., None] * bi + f_im[..., None] * br
    eye = jnp.eye(g, dtype=F32)
    blk_b = lambda t: jnp.einsum('lgnh,gk->lghkn', t, eye).reshape(nl, g * hh, g * n)
    ncol = g * n // LANES
    bb = jnp.stack([blk_b(bb_re), blk_b(bb_im)], axis=2).reshape(nl, g * hh, 2, ncol, LANES)
    bb = bb.transpose(0, 1, 3, 2, 4).reshape(nl, g * hh, 2 * g * n).astype(BF16)
    blk_c = lambda t: jnp.einsum('lghn,gk->lgnkh', t, eye).reshape(nl, g * n, g * hh)
    cc = jnp.stack([blk_c(c_re.astype(F32)), -blk_c(c_im.astype(F32))], axis=1)
    cc = cc.reshape(nl, 2, ncol, LANES, g * hh).transpose(0, 2, 1, 3, 4).reshape(nl, 2 * g * n, g * hh).astype(BF16)

    def a_pow(kk):
        kk = kk.astype(F32)[None, :, None, None]
        m = jnp.exp(kk * (lr * dt)[:, None])
        ph = kk * (li * dt)[:, None]
        return (m * jnp.cos(ph)).reshape(nl, -1, g * n), (m * jnp.sin(ph)).reshape(nl, -1, g * n)

    rows = jnp.arange(SUBLANES)
    tabs = []
    for sh in (1, 2, 4):
        pr_, pi_ = a_pow(jnp.full((SUBLANES,), sh))
        keep = (rows >= sh).astype(F32)[None, :, None]
        tabs += [pr_ * keep, pi_ * keep]
    pr_, pi_ = a_pow(rows + 1)
    tabs += [pr_, pi_]
    return bb, cc, jnp.stack(tabs, axis=1)


def _sink_rows(attn_sinks):
    group = N_Q_HEADS // N_KV_HEADS
    out = []
    for kh in range(N_KV_HEADS):
        for par in range(2):
            a = attn_sinks[:, group * kh + par]
            c = attn_sinks[:, group * kh + 2 + par]
            col = jnp.concatenate([jnp.broadcast_to(a[:, None], (a.shape[0], ATT_BLOCK)),
                                   jnp.broadcast_to(c[:, None], (a.shape[0], ATT_BLOCK))], axis=1)
            out.append(col)
    return jnp.stack(out, axis=1)[..., None].astype(F32)


def _fold_half(t, lo=None, hi=None):
    t = t.astype(F32)
    if lo is None:
        return 0.5 * t
    colscale = np.ones((t.shape[-1],), np.float32)
    colscale[lo:hi] = 0.5
    return t * jnp.asarray(colscale)


def kernel(x, p, positions, mix_norm_g, w_in, b_gate, attn_sinks, w_attn_out, ssm_lambda_re, ssm_lambda_im, ssm_log_dt, ssm_b_re, ssm_b_im, ssm_c_re, ssm_c_im, ssm_d, w_ssm_glu, b_ssm_glu, conv_dw_w, conv_dw_b, conv_norm_g, conv_norm_b, w_conv_out, w_mix_out, ffn_norm_g, w_ffn_in, w_ffn_out, w_ple_in, ple_norm_g, w_ple_gate, final_norm_g):
    b, s, d = x.shape
    depth = w_in.shape[0]
    ts = min(MIXER_TILE, s)
    tm = min(FFN_TILE, b * s)
    cos_t, sin_t = _rope_tables(positions)
    bb, cc, stab = _ssm_tables(ssm_lambda_re, ssm_lambda_im, ssm_log_dt, ssm_b_re, ssm_b_im, ssm_c_re, ssm_c_im)
    row = lambda t: t.astype(F32)[:, None, :]
    cw = conv_dw_w.shape[-1]
    o_c = w_in.shape[-1] - 3 * d - 2 * cw
    w_in_h = _fold_half(w_in, o_c, None)
    w_conv = w_in_h[..., o_c:o_c + 2 * cw].reshape(depth, d, 2, cw // LANES, LANES)
    w_conv = w_conv.transpose(0, 1, 3, 2, 4).reshape(depth, d, 2 * cw)
    w_in_h = jnp.concatenate([w_in_h[..., :o_c], w_conv, w_in_h[..., o_c + 2 * cw:]], axis=-1)
    mix = dict(
        mix_g=row(mix_norm_g), w_in=w_in_h.astype(BF16), b_gate=row(_fold_half(b_gate)),
        sinks=_sink_rows(attn_sinks), w_ao=w_attn_out.astype(BF16), bb=bb, stab=stab, cc=cc, dskip=row(ssm_d),
        w_glu=_fold_half(w_ssm_glu).astype(BF16), b_glu=row(_fold_half(b_ssm_glu)),
        dww=conv_dw_w.astype(F32), dwb=row(conv_dw_b),
        lng=row(_fold_half(conv_norm_g)), lnb=row(_fold_half(conv_norm_b)),
        w_co=w_conv_out.astype(BF16), w_mix=_fold_half(w_mix_out).astype(BF16))
    ffn = dict(
        ffn_g=row(ffn_norm_g), w_fi=w_ffn_in.astype(BF16), w_fo=w_ffn_out.astype(BF16),
        w_pi=w_ple_in.astype(BF16), ple_g=row(ple_norm_g), w_pg=w_ple_gate.astype(BF16))
    gfin = final_norm_g.astype(F32)[None, :]
    p2 = p.reshape(depth, b * s, p.shape[-1])
    for i in range(depth):
        x = _mixer_call(x, cos_t, sin_t, i, mix, ts)
        x = _ffn_call(x.reshape(1, b * s, d), p2, i, ffn, gfin, tm, final=(i == depth - 1)).reshape(b, s, d)
    return x
```

```python
import functools

import numpy as np
import jax
import jax.numpy as jnp
from jax import lax
from jax.experimental import pallas as pl
from jax.experimental.pallas import tpu as pltpu

EPS = 1e-6
NEG_INF = -1e30
HEAD_DIM = 64
N_Q_HEADS = 8
N_KV_HEADS = 2
ATT_BLOCK = 128
ROPE_DIM = HEAD_DIM // 4
ROPE_THETA = 500000.0
CONV_K = 31
CONV_HIST = 32
LANES = 128
SUBLANES = 8
VMEM_LIMIT_BYTES = 56 * 1024 * 1024

MIXER_TILE = 256
MIXER_SEQS = 2
FFN_TILE = 1024
FFN_CHUNK = 1024
GATE_CHUNK = 256

BF16 = jnp.bfloat16
F32 = jnp.float32


def _sigmoid(t):
    return 0.5 * jnp.tanh(0.5 * t) + 0.5


def _rms(t, g):
    return t * lax.rsqrt(jnp.mean(t * t, axis=-1, keepdims=True) + EPS) * g


_dot = functools.partial(jnp.dot, preferred_element_type=F32)
_dot_nt = functools.partial(lax.dot_general, dimension_numbers=(((1,), (1,)), ((), ())),
                            preferred_element_type=F32)


def _mixer_kernel(x_ref, cos_ref, sin_ref, g_ref, win_ref, bgate_ref, sink_ref, wao_ref,
                  bb_ref, stab_ref, cc_ref, dskip_ref, wglu_ref, bglu_ref,
                  dww_ref, dwb_ref, lng_ref, lnb_ref, wco_ref, wmix_ref,
                  o_ref,
                  kprev_ref, vprev_ref, cre_ref, cim_ref, cbuf_ref, ya_ref, zg_ref, st_ref,
                  *, nseq, ts, d, qw, kvw, sw, cw, ns):
    s = pl.program_id(1)
    nb = ts // ATT_BLOCK
    rows = nseq * ts
    o_q, o_kv, o_s, o_c, o_g = 0, qw, qw + 2 * kvw, qw + 2 * kvw + sw, qw + 2 * kvw + sw + 2 * cw

    @pl.when(s == 0)
    def _():
        kprev_ref[...] = jnp.zeros_like(kprev_ref)
        vprev_ref[...] = jnp.zeros_like(vprev_ref)
        cre_ref[...] = jnp.zeros_like(cre_ref)
        cim_ref[...] = jnp.zeros_like(cim_ref)
        cbuf_ref[:, 0:CONV_HIST, :] = jnp.zeros((nseq, CONV_HIST, cw), F32)

    x = x_ref[...].reshape(rows, d)
    h = _rms(x, g_ref[...]).astype(BF16)

    q = _dot(h, win_ref[:, o_q:o_q + qw])
    kv = _dot(h, win_ref[:, o_kv:o_kv + 2 * kvw])
    u = _dot(h, win_ref[:, o_s:o_s + sw])
    ub = u.astype(BF16)

    cos_t = cos_ref[...].reshape(rows, LANES)
    sin_t = sin_ref[...].reshape(rows, LANES)
    lane = lax.broadcasted_iota(jnp.int32, (rows, LANES), 1)
    lo8 = (lane & (HEAD_DIM - 1)) < (ROPE_DIM // 2)
    lo64 = lane < HEAD_DIM

    def rope(t):
        rot = jnp.where(lo8, pltpu.roll(t, LANES - ROPE_DIM // 2, 1), pltpu.roll(t, ROPE_DIM // 2, 1))
        return t * cos_t + rot * sin_t

    qb = [rope(q[:, c * LANES:(c + 1) * LANES]).astype(BF16) for c in range(qw // LANES)]
    k = rope(kv[:, 0:kvw])
    v = kv[:, kvw:2 * kvw]

    def variants(t):
        tr = pltpu.roll(t, HEAD_DIM, 1)
        zero = jnp.zeros_like(t)
        return [jnp.where(lo64, t, zero).astype(BF16), jnp.where(lo64, zero, tr).astype(BF16),
                jnp.where(lo64, tr, zero).astype(BF16), jnp.where(lo64, zero, t).astype(BF16)]

    kvar = variants(k)
    vvar = variants(v)

    def gate_chunk(i):
        zg_ref[:, i * GATE_CHUNK:(i + 1) * GATE_CHUNK] = _dot(
            h, win_ref[:, o_g + i * GATE_CHUNK:o_g + (i + 1) * GATE_CHUNK])

    blk2 = 2 * ATT_BLOCK
    row = lax.broadcasted_iota(jnp.int32, (blk2, 2 * blk2), 0) & (ATT_BLOCK - 1)
    col = lax.broadcasted_iota(jnp.int32, (blk2, 2 * blk2), 1) & (blk2 - 1)
    valid_cur = (col >= ATT_BLOCK) & ((col - ATT_BLOCK) <= row)
    valid_prev = (col < ATT_BLOCK) & (col > row)
    first_lim = jnp.where(s == 0, ATT_BLOCK, 0)
    valid_mid = valid_cur | valid_prev
    valid_first = valid_cur | (valid_prev & (col >= first_lim))
    lo64_2 = lax.broadcasted_iota(jnp.int32, (blk2, LANES), 1) < HEAD_DIM
    att = {}

    def keys(var, prev_ref, idx, sq, j):
        r0 = sq * ts + j * ATT_BLOCK
        prev = prev_ref[sq, idx] if j == 0 else var[idx][r0 - ATT_BLOCK:r0]
        return [prev, var[idx][r0:r0 + ATT_BLOCK]]

    def unit_of(unit):
        sq, rest = divmod(unit, nb * N_KV_HEADS)
        j, kh = divmod(rest, N_KV_HEADS)
        return sq, j, kh

    def att_qk(unit):
        sq, j, kh = unit_of(unit)
        r0 = sq * ts + j * ATT_BLOCK
        qs = jnp.concatenate([qb[2 * kh][r0:r0 + ATT_BLOCK], qb[2 * kh + 1][r0:r0 + ATT_BLOCK]], axis=0)
        kcat = jnp.concatenate(keys(kvar, kprev_ref, 2 * kh, sq, j) + keys(kvar, kprev_ref, 2 * kh + 1, sq, j), axis=0)
        att[unit] = _dot_nt(qs, kcat)

    def att_softmax(unit):
        sq, j, kh = unit_of(unit)
        sc = jnp.where(valid_first if j == 0 else valid_mid, att[unit], NEG_INF)
        probs, inv = [], []
        for par in range(2):
            sp = sc[:, par * blk2:(par + 1) * blk2]
            sink = sink_ref[2 * kh + par]
            m = jnp.maximum(jnp.max(sp, axis=-1, keepdims=True), sink)
            p = jnp.exp(sp - m)
            den = jnp.sum(p, axis=-1, keepdims=True) + jnp.exp(sink - m)
            probs.append(p.astype(BF16))
            inv.append(1.0 / den)
        att[unit] = (jnp.concatenate(probs, axis=1), jnp.where(lo64_2, inv[0], inv[1]))

    def att_pv(unit):
        sq, j, kh = unit_of(unit)
        r0 = sq * ts + j * ATT_BLOCK
        pcat, inv = att.pop(unit)
        vcat = jnp.concatenate(keys(vvar, vprev_ref, 2 * kh, sq, j) + keys(vvar, vprev_ref, 2 * kh + 1, sq, j), axis=0)
        o = _dot(pcat, vcat) * inv
        ya_ref[r0:r0 + ATT_BLOCK, (2 * kh) * LANES:(2 * kh + 1) * LANES] = o[0:ATT_BLOCK].astype(BF16)
        ya_ref[r0:r0 + ATT_BLOCK, (2 * kh + 1) * LANES:(2 * kh + 2) * LANES] = o[ATT_BLOCK:].astype(BF16)

    ncol = ns // LANES
    carry = [[[ref[sq, :, c * LANES:(c + 1) * LANES] for c in range(ncol)] for ref in (cre_ref, cim_ref)]
             for sq in range(nseq)]

    def scan_tile(bu_c, sq, r0, c):
        cs = slice(c * LANES, (c + 1) * LANES)
        xr = bu_c[r0:r0 + SUBLANES, 0:LANES]
        xi = bu_c[r0:r0 + SUBLANES, LANES:2 * LANES]
        for lvl, sh in enumerate((1, 2, 4)):
            ar = stab_ref[2 * lvl, :, cs]
            ai = stab_ref[2 * lvl + 1, :, cs]
            sr = pltpu.roll(xr, sh, 0)
            si = pltpu.roll(xi, sh, 0)
            xr, xi = xr + (ar * sr - ai * si), xi + (ar * si + ai * sr)
        pr = stab_ref[6, :, cs]
        pi = stab_ref[7, :, cs]
        cr, ci = carry[sq][0][c], carry[sq][1][c]
        gr = xr + (pr * cr - pi * ci)
        gi = xi + (pr * ci + pi * cr)
        carry[sq][0][c] = jnp.broadcast_to(gr[SUBLANES - 1:SUBLANES, :], (SUBLANES, LANES))
        carry[sq][1][c] = jnp.broadcast_to(gi[SUBLANES - 1:SUBLANES, :], (SUBLANES, LANES))
        return gr, gi

    def scan_chain(c):
        bu_c = _dot(ub, bb_ref[:, 2 * c * LANES:2 * (c + 1) * LANES])
        for r0 in range(0, rows, 2 * SUBLANES):
            sq = r0 // ts
            ar_, ai_ = scan_tile(bu_c, sq, r0, c)
            br_, bi_ = scan_tile(bu_c, sq, r0 + SUBLANES, c)
            st_ref[r0:r0 + 2 * SUBLANES, 2 * c * LANES:(2 * c + 1) * LANES] = (
                jnp.concatenate([ar_, br_], axis=0).astype(BF16))
            st_ref[r0:r0 + 2 * SUBLANES, (2 * c + 1) * LANES:(2 * c + 2) * LANES] = (
                jnp.concatenate([ai_, bi_], axis=0).astype(BF16))

    base = CONV_HIST - (CONV_K - 1)
    conv = []

    def conv_chunk(j):
        cs = slice(j * LANES, (j + 1) * LANES)
        c_in = _dot(h, win_ref[:, o_c + 2 * j * LANES:o_c + 2 * (j + 1) * LANES])
        ca = c_in[:, 0:LANES]
        cu = ca * jnp.tanh(c_in[:, LANES:2 * LANES]) + ca
        accs = []
        for sq in range(nseq):
            cbuf_ref[sq, CONV_HIST:CONV_HIST + ts, cs] = cu[sq * ts:(sq + 1) * ts]
            ext = cbuf_ref[sq, :, cs]
            acc = jnp.zeros((ts, LANES), F32) + dwb_ref[:, cs]
            for r in range(SUBLANES):
                e = pltpu.roll(ext, CONV_HIST + ts - r, 0) if r else ext
                for o in range(base, base + CONV_K):
                    if o % SUBLANES == r:
                        acc = acc + dww_ref[o - base:o - base + 1, cs] * e[o - r:o - r + ts]
            accs.append(acc)
        conv.append(jnp.concatenate(accs, axis=0))

    n_gate = 3 * d // GATE_CHUNK
    n_unit = nseq * nb * N_KV_HEADS
    n_conv = cw // LANES
    gate_ids = iter(range(n_gate))

    def next_gate():
        i = next(gate_ids, None)
        if i is not None:
            gate_chunk(i)

    for c in range(ncol):
        scan_chain(c)
        next_gate()
    for unit in range(n_unit):
        att_qk(unit)
        if unit % (n_unit // 2) == 0:
            next_gate()
    for unit in range(n_unit):
        att_softmax(unit)
    for unit in range(n_unit):
        att_pv(unit)
    for j in range(n_conv):
        conv_chunk(j)
        next_gate()
    for _ in range(n_gate):
        next_gate()

    for sq in range(nseq):
        for idx in range(4):
            kprev_ref[sq, idx] = kvar[idx][(sq + 1) * ts - ATT_BLOCK:(sq + 1) * ts]
            vprev_ref[sq, idx] = vvar[idx][(sq + 1) * ts - ATT_BLOCK:(sq + 1) * ts]
        cre_ref[sq] = jnp.concatenate(carry[sq][0], axis=1)
        cim_ref[sq] = jnp.concatenate(carry[sq][1], axis=1)
        cbuf_ref[sq, 0:CONV_HIST, :] = cbuf_ref[sq, ts:ts + CONV_HIST, :]

    def gate_tanh(bidx):
        return jnp.tanh(zg_ref[:, bidx * d:(bidx + 1) * d] + bgate_ref[:, bidx * d:(bidx + 1) * d])

    y_s = _dot(st_ref[...], cc_ref[...]) + dskip_ref[...] * u
    t1 = gate_tanh(1)
    glu = _dot(jax.nn.gelu(y_s).astype(BF16), wglu_ref[...]) + bglu_ref[...]
    acc = jnp.concatenate(conv, axis=1)
    mu = jnp.mean(acc, axis=-1, keepdims=True)
    cen = acc - mu
    var = jnp.mean(cen * cen, axis=-1, keepdims=True)
    ln = cen * lax.rsqrt(var + EPS) * lng_ref[...] + lnb_ref[...]
    y_conv = _dot((ln * jnp.tanh(ln) + ln).astype(BF16), wco_ref[...])
    t2 = gate_tanh(2)
    y_attn = _dot(ya_ref[...], wao_ref[...])
    t0 = gate_tanh(0)
    ga = glu[:, 0:d]
    y_ssm = ga * jnp.tanh(glu[:, d:2 * d]) + ga
    merged = (t0 * y_attn + y_attn) + (t1 * y_ssm + y_ssm) + (t2 * y_conv + y_conv)
    o_ref[...] = (x + _dot(merged.astype(BF16), wmix_ref[...])).reshape(nseq, ts, d)


def _const_spec(shape, layer=None):
    if layer is None:
        return pl.BlockSpec(shape, lambda *_: (0,) * len(shape), pipeline_mode=pl.Buffered(1))
    return pl.BlockSpec((None,) + shape, lambda *_: (layer,) + (0,) * len(shape), pipeline_mode=pl.Buffered(1))


def _mixer_call(x, cos_t, sin_t, layer, prm, ts, nseq):
    b, s, d = x.shape
    qw = N_Q_HEADS * HEAD_DIM
    kvw = N_KV_HEADS * HEAD_DIM
    sw = prm['bb'].shape[1]
    ns = prm['bb'].shape[2] // 2
    cw = prm['dww'].shape[2]
    inw = prm['w_in'].shape[2]
    assert s % ts == 0 and ts % ATT_BLOCK == 0 and b % nseq == 0
    assert kvw == LANES and qw % LANES == 0 and inw == qw + 2 * kvw + sw + 2 * cw + 3 * d
    assert (3 * d) % GATE_CHUNK == 0 and ns % LANES == 0 and cw % LANES == 0
    kern = functools.partial(_mixer_kernel, nseq=nseq, ts=ts, d=d, qw=qw, kvw=kvw, sw=sw, cw=cw, ns=ns)
    row_spec = lambda w: pl.BlockSpec((nseq, ts, w), lambda bi, si: (bi, si, 0))
    in_specs = [
        row_spec(d), row_spec(LANES), row_spec(LANES),
        _const_spec((1, d), layer),
        _const_spec((d, inw), layer),
        _const_spec((1, 3 * d), layer),
        _const_spec((4, 2 * ATT_BLOCK, 1), layer),
        _const_spec((qw, d), layer),
        _const_spec((sw, 2 * ns), layer),
        _const_spec((8, SUBLANES, ns), layer),
        _const_spec((2 * ns, sw), layer),
        _const_spec((1, sw), layer),
        _const_spec((sw, 2 * d), layer),
        _const_spec((1, 2 * d), layer),
        _const_spec((CONV_K, cw), layer),
        _const_spec((1, cw), layer),
        _const_spec((1, cw), layer),
        _const_spec((1, cw), layer),
        _const_spec((cw, d), layer),
        _const_spec((d, d), layer),
    ]
    scratch = [
        pltpu.VMEM((nseq, 4, ATT_BLOCK, LANES), BF16),
        pltpu.VMEM((nseq, 4, ATT_BLOCK, LANES), BF16),
        pltpu.VMEM((nseq, SUBLANES, ns), F32),
        pltpu.VMEM((nseq, SUBLANES, ns), F32),
        pltpu.VMEM((nseq, CONV_HIST + ts, cw), F32),
        pltpu.VMEM((nseq * ts, qw), BF16),
        pltpu.VMEM((nseq * ts, 3 * d), F32),
        pltpu.VMEM((nseq * ts, 2 * ns), BF16),
    ]
    return pl.pallas_call(
        kern,
        grid=(b // nseq, s // ts),
        in_specs=in_specs,
        out_specs=row_spec(d),
        out_shape=jax.ShapeDtypeStruct((b, s, d), F32),
        scratch_shapes=scratch,
        compiler_params=pltpu.CompilerParams(
            dimension_semantics=("arbitrary", "arbitrary"), vmem_limit_bytes=VMEM_LIMIT_BYTES),
        name=f"mixer_l{layer}",
    )(x, cos_t, sin_t, prm['mix_g'], prm['w_in'], prm['b_gate'], prm['sinks'], prm['w_ao'],
      prm['bb'], prm['stab'], prm['cc'], prm['dskip'], prm['w_glu'], prm['b_glu'],
      prm['dww'], prm['dwb'], prm['lng'], prm['lnb'], prm['w_co'], prm['w_mix'])


def _ffn_kernel(x_ref, p_ref, gf_ref, wfi_ref, wfo_ref, wpi_ref, gp_ref, wpg_ref, gfin_ref, o_ref,
                *, hidden, final):
    x = x_ref[...]
    hf = _rms(x, gf_ref[...]).astype(BF16)
    for a in range(0, hidden, FFN_CHUNK):
        b = min(a + FFN_CHUNK, hidden)
        gate = _dot(hf, wfi_ref[:, a:b])
        up = _dot(hf, wfi_ref[:, hidden + a:hidden + b])
        hid = (gate * _sigmoid(gate) * up).astype(BF16)
        x = x + _dot(hid, wfo_ref[a:b, :])
    e = _dot(p_ref[...].astype(BF16), wpi_ref[...])
    gp = _sigmoid(_dot(_rms(x, gp_ref[...]).astype(BF16), wpg_ref[...]))
    x = x + gp * e
    if final:
        x = _rms(x, gfin_ref[...])
    o_ref[...] = x


def _ffn_call(x2, p2, layer, prm, gfin, tm, final):
    n, d = x2.shape[1], x2.shape[2]
    pd = p2.shape[2]
    hidden = prm['w_fo'].shape[1]
    assert n % tm == 0
    kern = functools.partial(_ffn_kernel, hidden=hidden, final=final)
    in_specs = [
        pl.BlockSpec((None, tm, d), lambda i: (0, i, 0)),
        pl.BlockSpec((None, tm, pd), lambda i: (layer, i, 0)),
        _const_spec((1, d), layer),
        _const_spec((d, 2 * hidden), layer),
        _const_spec((hidden, d), layer),
        _const_spec((pd, d), layer),
        _const_spec((1, d), layer),
        _const_spec((d, d), layer),
        _const_spec((1, d)),
    ]
    return pl.pallas_call(
        kern,
        grid=(n // tm,),
        in_specs=in_specs,
        out_specs=pl.BlockSpec((None, tm, d), lambda i: (0, i, 0)),
        out_shape=jax.ShapeDtypeStruct((1, n, d), F32),
        compiler_params=pltpu.CompilerParams(
            dimension_semantics=("arbitrary",), vmem_limit_bytes=VMEM_LIMIT_BYTES),
        name=f"ffn_l{layer}",
    )(x2, p2, prm['ffn_g'], prm['w_fi'], prm['w_fo'], prm['w_pi'], prm['ple_g'], prm['w_pg'], gfin)


def _rope_tables(positions):
    half = ROPE_DIM // 2
    inv_freq = ROPE_THETA ** (-jnp.arange(0, ROPE_DIM, 2, dtype=F32) / ROPE_DIM)
    ang = positions.astype(F32)[..., None] * inv_freq
    lane = np.arange(LANES)
    in_rope = (lane % HEAD_DIM) < ROPE_DIM
    onehot = ((lane[None, :] % half) == np.arange(half)[:, None]) & in_rope[None, :]
    sign = np.where((lane % HEAD_DIM) < half, -1.0, 1.0)
    e_cos = jnp.asarray(onehot.astype(np.float32))
    e_sin = jnp.asarray((onehot * sign[None, :]).astype(np.float32))
    expand = lambda t, e: jnp.einsum('bsh,hl->bsl', t, e, precision=lax.Precision.HIGHEST)
    cos_t = expand(jnp.cos(ang), e_cos) + jnp.asarray((~in_rope).astype(np.float32))
    sin_t = expand(jnp.sin(ang), e_sin)
    return cos_t, sin_t


def _ssm_tables(lam_re, lam_im, log_dt, b_re, b_im, c_re, c_im):
    nl, g, n = lam_re.shape
    hh = b_re.shape[-1]
    lr = jnp.minimum(lam_re.astype(F32), -1e-4)
    li = lam_im.astype(F32)
    dt = jnp.exp(log_dt.astype(F32))[..., None]
    mag = jnp.exp(lr * dt)
    a_re = mag * jnp.cos(li * dt)
    a_im = mag * jnp.sin(li * dt)
    den = lr * lr + li * li
    x_re, x_im = a_re - 1.0, a_im
    f_re = (x_re * lr + x_im * li) / den
    f_im = (x_im * lr - x_re * li) / den
    br, bi = b_re.astype(F32), b_im.astype(F32)
    bb_re = f_re[..., None] * br - f_im[..., None] * bi
    bb_im = f_re[..., None] * bi + f_im[..., None] * br
    eye = jnp.eye(g, dtype=F32)
    blk_b = lambda t: jnp.einsum('lgnh,gk->lghkn', t, eye).reshape(nl, g * hh, g * n)
    ncol = g * n // LANES
    bb = jnp.stack([blk_b(bb_re), blk_b(bb_im)], axis=2).reshape(nl, g * hh, 2, ncol, LANES)
    bb = bb.transpose(0, 1, 3, 2, 4).reshape(nl, g * hh, 2 * g * n).astype(BF16)
    blk_c = lambda t: jnp.einsum('lghn,gk->lgnkh', t, eye).reshape(nl, g * n, g * hh)
    cc = jnp.stack([blk_c(c_re.astype(F32)), -blk_c(c_im.astype(F32))], axis=1)
    cc = cc.reshape(nl, 2, ncol, LANES, g * hh).transpose(0, 2, 1, 3, 4).reshape(nl, 2 * g * n, g * hh).astype(BF16)

    def a_pow(kk):
        kk = kk.astype(F32)[None, :, None, None]
        m = jnp.exp(kk * (lr * dt)[:, None])
        ph = kk * (li * dt)[:, None]
        return (m * jnp.cos(ph)).reshape(nl, -1, g * n), (m * jnp.sin(ph)).reshape(nl, -1, g * n)

    rows = jnp.arange(SUBLANES)
    tabs = []
    for sh in (1, 2, 4):
        pr_, pi_ = a_pow(jnp.full((SUBLANES,), sh))
        keep = (rows >= sh).astype(F32)[None, :, None]
        tabs += [pr_ * keep, pi_ * keep]
    pr_, pi_ = a_pow(rows + 1)
    tabs += [pr_, pi_]
    return bb, cc, jnp.stack(tabs, axis=1)


def _sink_rows(attn_sinks):
    group = N_Q_HEADS // N_KV_HEADS
    out = []
    for kh in range(N_KV_HEADS):
        for par in range(2):
            a = attn_sinks[:, group * kh + par]
            c = attn_sinks[:, group * kh + 2 + par]
            col = jnp.concatenate([jnp.broadcast_to(a[:, None], (a.shape[0], ATT_BLOCK)),
                                   jnp.broadcast_to(c[:, None], (a.shape[0], ATT_BLOCK))], axis=1)
            out.append(col)
    return jnp.stack(out, axis=1)[..., None].astype(F32)


def _fold_half(t, lo=None, hi=None):
    t = t.astype(F32)
    if lo is None:
        return 0.5 * t
    colscale = np.ones((t.shape[-1],), np.float32)
    colscale[lo:hi] = 0.5
    return t * jnp.asarray(colscale)


def kernel(x, p, positions, mix_norm_g, w_in, b_gate, attn_sinks, w_attn_out, ssm_lambda_re, ssm_lambda_im, ssm_log_dt, ssm_b_re, ssm_b_im, ssm_c_re, ssm_c_im, ssm_d, w_ssm_glu, b_ssm_glu, conv_dw_w, conv_dw_b, conv_norm_g, conv_norm_b, w_conv_out, w_mix_out, ffn_norm_g, w_ffn_in, w_ffn_out, w_ple_in, ple_norm_g, w_ple_gate, final_norm_g):
    b, s, d = x.shape
    depth = w_in.shape[0]
    ts = min(MIXER_TILE, s)
    nseq = MIXER_SEQS if b % MIXER_SEQS == 0 else 1
    tm = min(FFN_TILE, b * s)
    cos_t, sin_t = _rope_tables(positions)
    bb, cc, stab = _ssm_tables(ssm_lambda_re, ssm_lambda_im, ssm_log_dt, ssm_b_re, ssm_b_im, ssm_c_re, ssm_c_im)
    row = lambda t: t.astype(F32)[:, None, :]
    cw = conv_dw_w.shape[-1]
    o_c = w_in.shape[-1] - 3 * d - 2 * cw
    w_in_h = _fold_half(w_in, o_c, None)
    q_scale = np.ones((w_in.shape[-1],), np.float32)
    q_scale[:N_Q_HEADS * HEAD_DIM] = HEAD_DIM ** -0.5
    w_in_h = w_in_h * jnp.asarray(q_scale)
    w_conv = w_in_h[..., o_c:o_c + 2 * cw].reshape(depth, d, 2, cw // LANES, LANES)
    w_conv = w_conv.transpose(0, 1, 3, 2, 4).reshape(depth, d, 2 * cw)
    w_in_h = jnp.concatenate([w_in_h[..., :o_c], w_conv, w_in_h[..., o_c + 2 * cw:]], axis=-1)
    mix = dict(
        mix_g=row(mix_norm_g), w_in=w_in_h.astype(BF16), b_gate=row(_fold_half(b_gate)),
        sinks=_sink_rows(attn_sinks), w_ao=w_attn_out.astype(BF16), bb=bb, stab=stab, cc=cc, dskip=row(ssm_d),
        w_glu=_fold_half(w_ssm_glu).astype(BF16), b_glu=row(_fold_half(b_ssm_glu)),
        dww=conv_dw_w.astype(F32), dwb=row(conv_dw_b),
        lng=row(_fold_half(conv_norm_g)), lnb=row(_fold_half(conv_norm_b)),
        w_co=w_conv_out.astype(BF16), w_mix=_fold_half(w_mix_out).astype(BF16))
    ffn = dict(
        ffn_g=row(ffn_norm_g), w_fi=w_ffn_in.astype(BF16), w_fo=w_ffn_out.astype(BF16),
        w_pi=w_ple_in.astype(BF16), ple_g=row(ple_norm_g), w_pg=w_ple_gate.astype(BF16))
    gfin = final_norm_g.astype(F32)[None, :]
    p2 = p.reshape(depth, b * s, p.shape[-1])
    for i in range(depth):
        x = _mixer_call(x, cos_t, sin_t, i, mix, ts, nseq)
        x = _ffn_call(x.reshape(1, b * s, d), p2, i, ffn, gfin, tm, final=(i == depth - 1)).reshape(b, s, d)
    return x
```

```python
import functools

import numpy as np
import jax
import jax.numpy as jnp
from jax import lax
from jax.experimental import pallas as pl
from jax.experimental.pallas import tpu as pltpu

EPS = 1e-6
NEG_INF = -1e30
HEAD_DIM = 64
N_Q_HEADS = 8
N_KV_HEADS = 2
ATT_BLOCK = 128
ROPE_DIM = HEAD_DIM // 4
ROPE_THETA = 500000.0
CONV_K = 31
CONV_HIST = 32
LANES = 128
SUBLANES = 8
VMEM_LIMIT_BYTES = 56 * 1024 * 1024

MIXER_TILE = 512
FFN_TILE = 1024
FFN_CHUNK = 1024
GATE_CHUNK = 256

BF16 = jnp.bfloat16
F32 = jnp.float32


def _sigmoid(t):
    return 0.5 * jnp.tanh(0.5 * t) + 0.5


def _rms(t, g):
    return t * lax.rsqrt(jnp.mean(t * t, axis=-1, keepdims=True) + EPS) * g


_dot = functools.partial(jnp.dot, preferred_element_type=F32)
_dot_nt = functools.partial(lax.dot_general, dimension_numbers=(((1,), (1,)), ((), ())),
                            preferred_element_type=F32)


def _mixer_kernel(x_ref, cos_ref, sin_ref, g_ref, win_ref, bgate_ref, sink_ref, wao_ref,
                  bb_ref, stab_ref, cc_ref, dskip_ref, wglu_ref, bglu_ref,
                  dww_ref, dwb_ref, lng_ref, lnb_ref, wco_ref, wmix_ref,
                  o_ref,
                  kprev_ref, vprev_ref, cre_ref, cim_ref, cbuf_ref, ya_ref, zg_ref, st_ref,
                  *, ts, d, qw, kvw, sw, cw, ns):
    s = pl.program_id(1)
    nb = ts // ATT_BLOCK
    o_q, o_kv, o_s, o_c, o_g = 0, qw, qw + 2 * kvw, qw + 2 * kvw + sw, qw + 2 * kvw + sw + 2 * cw

    @pl.when(s == 0)
    def _():
        kprev_ref[...] = jnp.zeros_like(kprev_ref)
        vprev_ref[...] = jnp.zeros_like(vprev_ref)
        cre_ref[...] = jnp.zeros_like(cre_ref)
        cim_ref[...] = jnp.zeros_like(cim_ref)
        cbuf_ref[0:CONV_HIST, :] = jnp.zeros((CONV_HIST, cw), F32)

    x = x_ref[...]
    h = _rms(x, g_ref[...]).astype(BF16)

    q = _dot(h, win_ref[:, o_q:o_q + qw])
    kv = _dot(h, win_ref[:, o_kv:o_kv + 2 * kvw])
    u = _dot(h, win_ref[:, o_s:o_s + sw])
    ub = u.astype(BF16)

    cos_t = cos_ref[...]
    sin_t = sin_ref[...]
    lane = lax.broadcasted_iota(jnp.int32, (ts, LANES), 1)
    lo8 = (lane & (HEAD_DIM - 1)) < (ROPE_DIM // 2)
    lo64 = lane < HEAD_DIM

    def rope(t):
        rot = jnp.where(lo8, pltpu.roll(t, LANES - ROPE_DIM // 2, 1), pltpu.roll(t, ROPE_DIM // 2, 1))
        return t * cos_t + rot * sin_t

    scale = HEAD_DIM ** -0.5
    qb = [(rope(q[:, c * LANES:(c + 1) * LANES]) * scale).astype(BF16) for c in range(qw // LANES)]
    k = rope(kv[:, 0:kvw])
    v = kv[:, kvw:2 * kvw]

    def variants(t):
        tr = pltpu.roll(t, HEAD_DIM, 1)
        zero = jnp.zeros_like(t)
        return [jnp.where(lo64, t, zero).astype(BF16), jnp.where(lo64, zero, tr).astype(BF16),
                jnp.where(lo64, tr, zero).astype(BF16), jnp.where(lo64, zero, t).astype(BF16)]

    kvar = variants(k)
    vvar = variants(v)

    def gate_chunk(i):
        zg_ref[:, i * GATE_CHUNK:(i + 1) * GATE_CHUNK] = _dot(
            h, win_ref[:, o_g + i * GATE_CHUNK:o_g + (i + 1) * GATE_CHUNK])

    blk2 = 2 * ATT_BLOCK
    row = lax.broadcasted_iota(jnp.int32, (blk2, 2 * blk2), 0) & (ATT_BLOCK - 1)
    col = lax.broadcasted_iota(jnp.int32, (blk2, 2 * blk2), 1) & (blk2 - 1)
    valid_cur = (col >= ATT_BLOCK) & ((col - ATT_BLOCK) <= row)
    valid_prev = (col < ATT_BLOCK) & (col > row)
    first_lim = jnp.where(s == 0, ATT_BLOCK, 0)
    valid_mid = valid_cur | valid_prev
    valid_first = valid_cur | (valid_prev & (col >= first_lim))
    lo64_2 = lax.broadcasted_iota(jnp.int32, (blk2, LANES), 1) < HEAD_DIM
    att = {}

    def keys(var, prev_ref, idx, j):
        r0 = j * ATT_BLOCK
        prev = prev_ref[idx] if j == 0 else var[idx][r0 - ATT_BLOCK:r0]
        return [prev, var[idx][r0:r0 + ATT_BLOCK]]

    def att_qk(unit):
        j, kh = divmod(unit, N_KV_HEADS)
        r0 = j * ATT_BLOCK
        qs = jnp.concatenate([qb[2 * kh][r0:r0 + ATT_BLOCK], qb[2 * kh + 1][r0:r0 + ATT_BLOCK]], axis=0)
        kcat = jnp.concatenate(keys(kvar, kprev_ref, 2 * kh, j) + keys(kvar, kprev_ref, 2 * kh + 1, j), axis=0)
        att[unit] = _dot_nt(qs, kcat)

    def att_softmax(unit):
        j, kh = divmod(unit, N_KV_HEADS)
        sc = jnp.where(valid_first if j == 0 else valid_mid, att[unit], NEG_INF)
        probs, inv = [], []
        for par in range(2):
            sp = sc[:, par * blk2:(par + 1) * blk2]
            sink = sink_ref[2 * kh + par]
            m = jnp.maximum(jnp.max(sp, axis=-1, keepdims=True), sink)
            p = jnp.exp(sp - m)
            den = jnp.sum(p, axis=-1, keepdims=True) + jnp.exp(sink - m)
            probs.append(p.astype(BF16))
            inv.append(1.0 / den)
        att[unit] = (jnp.concatenate(probs, axis=1), jnp.where(lo64_2, inv[0], inv[1]))

    def att_pv(unit):
        j, kh = divmod(unit, N_KV_HEADS)
        r0 = j * ATT_BLOCK
        pcat, inv = att.pop(unit)
        vcat = jnp.concatenate(keys(vvar, vprev_ref, 2 * kh, j) + keys(vvar, vprev_ref, 2 * kh + 1, j), axis=0)
        o = _dot(pcat, vcat) * inv
        ya_ref[r0:r0 + ATT_BLOCK, (2 * kh) * LANES:(2 * kh + 1) * LANES] = o[0:ATT_BLOCK].astype(BF16)
        ya_ref[r0:r0 + ATT_BLOCK, (2 * kh + 1) * LANES:(2 * kh + 2) * LANES] = o[ATT_BLOCK:].astype(BF16)

    ncol = ns // LANES
    carry = [[cre_ref[:, c * LANES:(c + 1) * LANES] for c in range(ncol)],
             [cim_ref[:, c * LANES:(c + 1) * LANES] for c in range(ncol)]]

    def scan_tile(bu_c, r0, c):
        cs = slice(c * LANES, (c + 1) * LANES)
        xr = bu_c[r0:r0 + SUBLANES, 0:LANES]
        xi = bu_c[r0:r0 + SUBLANES, LANES:2 * LANES]
        for lvl, sh in enumerate((1, 2, 4)):
            ar = stab_ref[2 * lvl, :, cs]
            ai = stab_ref[2 * lvl + 1, :, cs]
            sr = pltpu.roll(xr, sh, 0)
            si = pltpu.roll(xi, sh, 0)
            xr, xi = xr + (ar * sr - ai * si), xi + (ar * si + ai * sr)
        pr = stab_ref[6, :, cs]
        pi = stab_ref[7, :, cs]
        cr, ci = carry[0][c], carry[1][c]
        gr = xr + (pr * cr - pi * ci)
        gi = xi + (pr * ci + pi * cr)
        carry[0][c] = jnp.broadcast_to(gr[SUBLANES - 1:SUBLANES, :], (SUBLANES, LANES))
        carry[1][c] = jnp.broadcast_to(gi[SUBLANES - 1:SUBLANES, :], (SUBLANES, LANES))
        return gr, gi

    def scan_chain(c):
        bu_c = _dot(ub, bb_ref[:, 2 * c * LANES:2 * (c + 1) * LANES])
        for r0 in range(0, ts, 2 * SUBLANES):
            ar_, ai_ = scan_tile(bu_c, r0, c)
            br_, bi_ = scan_tile(bu_c, r0 + SUBLANES, c)
            st_ref[r0:r0 + 2 * SUBLANES, 2 * c * LANES:(2 * c + 1) * LANES] = (
                jnp.concatenate([ar_, br_], axis=0).astype(BF16))
            st_ref[r0:r0 + 2 * SUBLANES, (2 * c + 1) * LANES:(2 * c + 2) * LANES] = (
                jnp.concatenate([ai_, bi_], axis=0).astype(BF16))

    base = CONV_HIST - (CONV_K - 1)
    conv = []

    def conv_chunk(j):
        cs = slice(j * LANES, (j + 1) * LANES)
        c_in = _dot(h, win_ref[:, o_c + 2 * j * LANES:o_c + 2 * (j + 1) * LANES])
        ca = c_in[:, 0:LANES]
        cbuf_ref[CONV_HIST:CONV_HIST + ts, cs] = ca * jnp.tanh(c_in[:, LANES:2 * LANES]) + ca
        ext = cbuf_ref[:, cs]
        acc = jnp.zeros((ts, LANES), F32) + dwb_ref[:, cs]
        for r in range(SUBLANES):
            e = pltpu.roll(ext, CONV_HIST + ts - r, 0) if r else ext
            for o in range(base, base + CONV_K):
                if o % SUBLANES == r:
                    acc = acc + dww_ref[o - base:o - base + 1, cs] * e[o - r:o - r + ts]
        conv.append(acc)

    n_gate = 3 * d // GATE_CHUNK
    n_unit = nb * N_KV_HEADS
    n_conv = cw // LANES
    gate_ids = iter(range(n_gate))

    def next_gate():
        i = next(gate_ids, None)
        if i is not None:
            gate_chunk(i)

    for c in range(ncol):
        scan_chain(c)
        next_gate()
    for unit in range(n_unit):
        att_qk(unit)
    for unit in range(n_unit):
        att_softmax(unit)
    for j in range(n_conv):
        conv_chunk(j)
        for _ in range((n_gate - ncol) // n_conv):
            next_gate()
    for _ in range(n_gate):
        next_gate()
    for unit in range(n_unit):
        att_pv(unit)

    for idx in range(4):
        kprev_ref[idx] = kvar[idx][ts - ATT_BLOCK:ts]
        vprev_ref[idx] = vvar[idx][ts - ATT_BLOCK:ts]
    cre_ref[...] = jnp.concatenate(carry[0], axis=1)
    cim_ref[...] = jnp.concatenate(carry[1], axis=1)
    cbuf_ref[0:CONV_HIST, :] = cbuf_ref[ts:ts + CONV_HIST, :]

    def gate_tanh(bidx):
        return jnp.tanh(zg_ref[:, bidx * d:(bidx + 1) * d] + bgate_ref[:, bidx * d:(bidx + 1) * d])

    y_s = _dot(st_ref[...], cc_ref[...]) + dskip_ref[...] * u
    t1 = gate_tanh(1)
    glu = _dot(jax.nn.gelu(y_s).astype(BF16), wglu_ref[...]) + bglu_ref[...]
    acc = jnp.concatenate(conv, axis=1)
    mu = jnp.mean(acc, axis=-1, keepdims=True)
    cen = acc - mu
    var = jnp.mean(cen * cen, axis=-1, keepdims=True)
    ln = cen * lax.rsqrt(var + EPS) * lng_ref[...] + lnb_ref[...]
    y_conv = _dot((ln * jnp.tanh(ln) + ln).astype(BF16), wco_ref[...])
    t2 = gate_tanh(2)
    y_attn = _dot(ya_ref[...], wao_ref[...])
    t0 = gate_tanh(0)
    ga = glu[:, 0:d]
    y_ssm = ga * jnp.tanh(glu[:, d:2 * d]) + ga
    merged = (t0 * y_attn + y_attn) + (t1 * y_ssm + y_ssm) + (t2 * y_conv + y_conv)
    o_ref[...] = x + _dot(merged.astype(BF16), wmix_ref[...])


def _const_spec(shape, layer=None):
    if layer is None:
        return pl.BlockSpec(shape, lambda *_: (0,) * len(shape), pipeline_mode=pl.Buffered(1))
    return pl.BlockSpec((None,) + shape, lambda *_: (layer,) + (0,) * len(shape), pipeline_mode=pl.Buffered(1))


def _mixer_call(x, cos_t, sin_t, layer, prm, ts):
    b, s, d = x.shape
    qw = N_Q_HEADS * HEAD_DIM
    kvw = N_KV_HEADS * HEAD_DIM
    sw = prm['bb'].shape[1]
    ns = prm['bb'].shape[2] // 2
    cw = prm['dww'].shape[2]
    inw = prm['w_in'].shape[2]
    assert s % ts == 0 and ts % ATT_BLOCK == 0
    assert kvw == LANES and qw % LANES == 0 and inw == qw + 2 * kvw + sw + 2 * cw + 3 * d
    assert (3 * d) % GATE_CHUNK == 0 and ns % LANES == 0 and cw % LANES == 0
    kern = functools.partial(_mixer_kernel, ts=ts, d=d, qw=qw, kvw=kvw, sw=sw, cw=cw, ns=ns)
    row_spec = lambda w: pl.BlockSpec((None, ts, w), lambda bi, si: (bi, si, 0))
    in_specs = [
        row_spec(d), row_spec(LANES), row_spec(LANES),
        _const_spec((1, d), layer),
        _const_spec((d, inw), layer),
        _const_spec((1, 3 * d), layer),
        _const_spec((4, 2 * ATT_BLOCK, 1), layer),
        _const_spec((qw, d), layer),
        _const_spec((sw, 2 * ns), layer),
        _const_spec((8, SUBLANES, ns), layer),
        _const_spec((2 * ns, sw), layer),
        _const_spec((1, sw), layer),
        _const_spec((sw, 2 * d), layer),
        _const_spec((1, 2 * d), layer),
        _const_spec((CONV_K, cw), layer),
        _const_spec((1, cw), layer),
        _const_spec((1, cw), layer),
        _const_spec((1, cw), layer),
        _const_spec((cw, d), layer),
        _const_spec((d, d), layer),
    ]
    scratch = [
        pltpu.VMEM((4, ATT_BLOCK, LANES), BF16),
        pltpu.VMEM((4, ATT_BLOCK, LANES), BF16),
        pltpu.VMEM((SUBLANES, ns), F32),
        pltpu.VMEM((SUBLANES, ns), F32),
        pltpu.VMEM((CONV_HIST + ts, cw), F32),
        pltpu.VMEM((ts, qw), BF16),
        pltpu.VMEM((ts, 3 * d), F32),
        pltpu.VMEM((ts, 2 * ns), BF16),
    ]
    return pl.pallas_call(
        kern,
        grid=(b, s // ts),
        in_specs=in_specs,
        out_specs=pl.BlockSpec((None, ts, d), lambda bi, si: (bi, si, 0)),
        out_shape=jax.ShapeDtypeStruct((b, s, d), F32),
        scratch_shapes=scratch,
        compiler_params=pltpu.CompilerParams(
            dimension_semantics=("arbitrary", "arbitrary"), vmem_limit_bytes=VMEM_LIMIT_BYTES),
        name=f"mixer_l{layer}",
    )(x, cos_t, sin_t, prm['mix_g'], prm['w_in'], prm['b_gate'], prm['sinks'], prm['w_ao'],
      prm['bb'], prm['stab'], prm['cc'], prm['dskip'], prm['w_glu'], prm['b_glu'],
      prm['dww'], prm['dwb'], prm['lng'], prm['lnb'], prm['w_co'], prm['w_mix'])


def _ffn_kernel(x_ref, p_ref, gf_ref, wfi_ref, wfo_ref, wpi_ref, gp_ref, wpg_ref, gfin_ref, o_ref,
                *, hidden, final):
    x = x_ref[...]
    hf = _rms(x, gf_ref[...]).astype(BF16)
    for a in range(0, hidden, FFN_CHUNK):
        b = min(a + FFN_CHUNK, hidden)
        gate = _dot(hf, wfi_ref[:, a:b])
        up = _dot(hf, wfi_ref[:, hidden + a:hidden + b])
        hid = (gate * _sigmoid(gate) * up).astype(BF16)
        x = x + _dot(hid, wfo_ref[a:b, :])
    e = _dot(p_ref[...].astype(BF16), wpi_ref[...])
    gp = _sigmoid(_dot(_rms(x, gp_ref[...]).astype(BF16), wpg_ref[...]))
    x = x + gp * e
    if final:
        x = _rms(x, gfin_ref[...])
    o_ref[...] = x


def _ffn_call(x2, p2, layer, prm, gfin, tm, final):
    n, d = x2.shape[1], x2.shape[2]
    pd = p2.shape[2]
    hidden = prm['w_fo'].shape[1]
    assert n % tm == 0
    kern = functools.partial(_ffn_kernel, hidden=hidden, final=final)
    in_specs = [
        pl.BlockSpec((None, tm, d), lambda i: (0, i, 0)),
        pl.BlockSpec((None, tm, pd), lambda i: (layer, i, 0)),
        _const_spec((1, d), layer),
        _const_spec((d, 2 * hidden), layer),
        _const_spec((hidden, d), layer),
        _const_spec((pd, d), layer),
        _const_spec((1, d), layer),
        _const_spec((d, d), layer),
        _const_spec((1, d)),
    ]
    return pl.pallas_call(
        kern,
        grid=(n // tm,),
        in_specs=in_specs,
        out_specs=pl.BlockSpec((None, tm, d), lambda i: (0, i, 0)),
        out_shape=jax.ShapeDtypeStruct((1, n, d), F32),
        compiler_params=pltpu.CompilerParams(
            dimension_semantics=("arbitrary",), vmem_limit_bytes=VMEM_LIMIT_BYTES),
        name=f"ffn_l{layer}",
    )(x2, p2, prm['ffn_g'], prm['w_fi'], prm['w_fo'], prm['w_pi'], prm['ple_g'], prm['w_pg'], gfin)


def _rope_tables(positions):
    half = ROPE_DIM // 2
    inv_freq = ROPE_THETA ** (-jnp.arange(0, ROPE_DIM, 2, dtype=F32) / ROPE_DIM)
    ang = positions.astype(F32)[..., None] * inv_freq
    lane = np.arange(LANES)
    in_rope = (lane % HEAD_DIM) < ROPE_DIM
    onehot = ((lane[None, :] % half) == np.arange(half)[:, None]) & in_rope[None, :]
    sign = np.where((lane % HEAD_DIM) < half, -1.0, 1.0)
    e_cos = jnp.asarray(onehot.astype(np.float32))
    e_sin = jnp.asarray((onehot * sign[None, :]).astype(np.float32))
    expand = lambda t, e: jnp.einsum('bsh,hl->bsl', t, e, precision=lax.Precision.HIGHEST)
    cos_t = expand(jnp.cos(ang), e_cos) + jnp.asarray((~in_rope).astype(np.float32))
    sin_t = expand(jnp.sin(ang), e_sin)
    return cos_t, sin_t


def _ssm_tables(lam_re, lam_im, log_dt, b_re, b_im, c_re, c_im):
    nl, g, n = lam_re.shape
    hh = b_re.shape[-1]
    lr = jnp.minimum(lam_re.astype(F32), -1e-4)
    li = lam_im.astype(F32)
    dt = jnp.exp(log_dt.astype(F32))[..., None]
    mag = jnp.exp(lr * dt)
    a_re = mag * jnp.cos(li * dt)
    a_im = mag * jnp.sin(li * dt)
    den = lr * lr + li * li
    x_re, x_im = a_re - 1.0, a_im
    f_re = (x_re * lr + x_im * li) / den
    f_im = (x_im * lr - x_re * li) / den
    br, bi = b_re.astype(F32), b_im.astype(F32)
    bb_re = f_re[..., None] * br - f_im[..., None] * bi
    bb_im = f_re[..., None] * bi + f_im[..., None] * br
    eye = jnp.eye(g, dtype=F32)
    blk_b = lambda t: jnp.einsum('lgnh,gk->lghkn', t, eye).reshape(nl, g * hh, g * n)
    ncol = g * n // LANES
    bb = jnp.stack([blk_b(bb_re), blk_b(bb_im)], axis=2).reshape(nl, g * hh, 2, ncol, LANES)
    bb = bb.transpose(0, 1, 3, 2, 4).reshape(nl, g * hh, 2 * g * n).astype(BF16)
    blk_c = lambda t: jnp.einsum('lghn,gk->lgnkh', t, eye).reshape(nl, g * n, g * hh)
    cc = jnp.stack([blk_c(c_re.astype(F32)), -blk_c(c_im.astype(F32))], axis=1)
    cc = cc.reshape(nl, 2, ncol, LANES, g * hh).transpose(0, 2, 1, 3, 4).reshape(nl, 2 * g * n, g * hh).astype(BF16)

    def a_pow(kk):
        kk = kk.astype(F32)[None, :, None, None]
        m = jnp.exp(kk * (lr * dt)[:, None])
        ph = kk * (li * dt)[:, None]
        return (m * jnp.cos(ph)).reshape(nl, -1, g * n), (m * jnp.sin(ph)).reshape(nl, -1, g * n)

    rows = jnp.arange(SUBLANES)
    tabs = []
    for sh in (1, 2, 4):
        pr_, pi_ = a_pow(jnp.full((SUBLANES,), sh))
        keep = (rows >= sh).astype(F32)[None, :, None]
        tabs += [pr_ * keep, pi_ * keep]
    pr_, pi_ = a_pow(rows + 1)
    tabs += [pr_, pi_]
    return bb, cc, jnp.stack(tabs, axis=1)


def _sink_rows(attn_sinks):
    group = N_Q_HEADS // N_KV_HEADS
    out = []
    for kh in range(N_KV_HEADS):
        for par in range(2):
            a = attn_sinks[:, group * kh + par]
            c = attn_sinks[:, group * kh + 2 + par]
            col = jnp.concatenate([jnp.broadcast_to(a[:, None], (a.shape[0], ATT_BLOCK)),
                                   jnp.broadcast_to(c[:, None], (a.shape[0], ATT_BLOCK))], axis=1)
            out.append(col)
    return jnp.stack(out, axis=1)[..., None].astype(F32)


def _fold_half(t, lo=None, hi=None):
    t = t.astype(F32)
    if lo is None:
        return 0.5 * t
    colscale = np.ones((t.shape[-1],), np.float32)
    colscale[lo:hi] = 0.5
    return t * jnp.asarray(colscale)


def kernel(x, p, positions, mix_norm_g, w_in, b_gate, attn_sinks, w_attn_out, ssm_lambda_re, ssm_lambda_im, ssm_log_dt, ssm_b_re, ssm_b_im, ssm_c_re, ssm_c_im, ssm_d, w_ssm_glu, b_ssm_glu, conv_dw_w, conv_dw_b, conv_norm_g, conv_norm_b, w_conv_out, w_mix_out, ffn_norm_g, w_ffn_in, w_ffn_out, w_ple_in, ple_norm_g, w_ple_gate, final_norm_g):
    b, s, d = x.shape
    depth = w_in.shape[0]
    ts = min(MIXER_TILE, s)
    tm = min(FFN_TILE, b * s)
    cos_t, sin_t = _rope_tables(positions)
    bb, cc, stab = _ssm_tables(ssm_lambda_re, ssm_lambda_im, ssm_log_dt, ssm_b_re, ssm_b_im, ssm_c_re, ssm_c_im)
    row = lambda t: t.astype(F32)[:, None, :]
    cw = conv_dw_w.shape[-1]
    o_c = w_in.shape[-1] - 3 * d - 2 * cw
    w_in_h = _fold_half(w_in, o_c, None)
    w_conv = w_in_h[..., o_c:o_c + 2 * cw].reshape(depth, d, 2, cw // LANES, LANES)
    w_conv = w_conv.transpose(0, 1, 3, 2, 4).reshape(depth, d, 2 * cw)
    w_in_h = jnp.concatenate([w_in_h[..., :o_c], w_conv, w_in_h[..., o_c + 2 * cw:]], axis=-1)
    mix = dict(
        mix_g=row(mix_norm_g), w_in=w_in_h.astype(BF16), b_gate=row(_fold_half(b_gate)),
        sinks=_sink_rows(attn_sinks), w_ao=w_attn_out.astype(BF16), bb=bb, stab=stab, cc=cc, dskip=row(ssm_d),
        w_glu=_fold_half(w_ssm_glu).astype(BF16), b_glu=row(_fold_half(b_ssm_glu)),
        dww=conv_dw_w.astype(F32), dwb=row(conv_dw_b),
        lng=row(_fold_half(conv_norm_g)), lnb=row(_fold_half(conv_norm_b)),
        w_co=w_conv_out.astype(BF16), w_mix=_fold_half(w_mix_out).astype(BF16))
    ffn = dict(
        ffn_g=row(ffn_norm_g), w_fi=w_ffn_in.astype(BF16), w_fo=w_ffn_out.astype(BF16),
        w_pi=w_ple_in.astype(BF16), ple_g=row(ple_norm_g), w_pg=w_ple_gate.astype(BF16))
    gfin = final_norm_g.astype(F32)[None, :]
    p2 = p.reshape(depth, b * s, p.shape[-1])
    for i in range(depth):
        x = _mixer_call(x, cos_t, sin_t, i, mix, ts)
        x = _ffn_call(x.reshape(1, b * s, d), p2, i, ffn, gfin, tm, final=(i == depth - 1)).reshape(b, s, d)
    return x
```
